```python
import jax
import jax.numpy as jnp
from jax import lax
import numpy as np

D_MODEL = 1024
BATCH = 2
SEQ = 8192
DEPTH = 1

CTX_LEN = 256
GRID_W = 64
D_MIX = D_MODEL
RW_WIDTH = D_MIX // 2
RW_HEAD = 64
RW_HEADS = RW_WIDTH // RW_HEAD
DECAY_LORA = 64
AAA_LORA = 64
MLA_WIDTH = D_MIX - RW_WIDTH
MLA_HEADS = 8
QK_NOPE = 64
QK_ROPE = 32
QK_DIM = QK_NOPE + QK_ROPE
V_HEAD = MLA_WIDTH // MLA_HEADS
Q_LORA = 384
KV_LORA = 256
AXIS_DIM = QK_ROPE // 2
ROPE_THETA = 10000.0
Q_BLOCK = 128
NORM_EPS = 1e-6
LNX_EPS = 64e-5
ATTN_SCALE = QK_DIM ** -0.5
N_SHIFT = 3 * RW_WIDTH + 2 * DECAY_LORA + 2 * AAA_LORA
MLA_LAT = Q_LORA + KV_LORA + QK_ROPE
D_IN = N_SHIFT + RW_WIDTH + MLA_LAT + MLA_WIDTH
SPLITS = (N_SHIFT, N_SHIFT + RW_WIDTH, N_SHIFT + RW_WIDTH + MLA_LAT)

kernel_name = 'hymba_rwkv7_mla_dit_block'


def rms_norm(x, g):
    xf = x.astype(jnp.float32)
    y = xf * lax.rsqrt(jnp.mean(xf * xf, axis=-1, keepdims=True) + NORM_EPS)
    return (y * g.astype(jnp.float32)).astype(x.dtype)


def token_shift(u, mu_prev, mu_next):
    prev = jnp.pad(u[:, :-1], ((0, 0), (1, 0), (0, 0)))
    nxt = jnp.pad(u[:, 1:], ((0, 0), (0, 1), (0, 0)))
    return u + mu_prev * (prev - u) + mu_next * (nxt - u)


def to_heads(t):
    return t.reshape(t.shape[:-1] + (RW_HEADS, RW_HEAD))


def rwkv_prep(u_shift, mu, w0, w2, a0, a2, k_k, k_a):
    us = token_shift(u_shift, mu[0], mu[1])
    r, k, v, w_in, a_in = jnp.split(
        us, [RW_WIDTH, 2 * RW_WIDTH, 3 * RW_WIDTH, 3 * RW_WIDTH + 2 * DECAY_LORA], axis=-1)
    B, L = us.shape[:2]
    w_in = w_in.reshape(B, L, 2, DECAY_LORA)
    a_in = a_in.reshape(B, L, 2, AAA_LORA)
    w_log = -jax.nn.softplus(-(w0[:, None, None, :] + jnp.einsum('blde,def->dblf', jnp.tanh(w_in), w2))) - 0.5
    decay = jnp.exp(-jnp.exp(w_log.astype(jnp.float32)))
    a = jax.nn.sigmoid(a0[:, None, None, :] + jnp.einsum('blde,def->dblf', a_in, a2))
    kkf = to_heads(k * k_k).astype(jnp.float32)
    kk = (kkf * lax.rsqrt(jnp.maximum(jnp.sum(kkf * kkf, -1, keepdims=True), 1e-24))).astype(k.dtype)
    a = to_heads(a)
    k_dir = to_heads(k)[None] * (1 + (a - 1) * to_heads(k_a))
    b_dir = kk[None] * a
    return to_heads(r), kk, to_heads(v), to_heads(decay), k_dir, b_dir


def rwkv_scan(state0, r, decay, k, v, a_neg, b, reverse, emit):
    xs = tuple(jnp.moveaxis(t.astype(jnp.float32), 1, 0) for t in (r, decay, k, v, a_neg, b))

    def step(S, inp):
        r_t, w_t, k_t, v_t, a_t, b_t = inp
        sa = jnp.einsum('bhij,bhj->bhi', S, a_t)
        S = S * w_t[:, :, None, :] + sa[..., None] * b_t[:, :, None, :] + v_t[..., None] * k_t[:, :, None, :]
        y = jnp.einsum('bhij,bhj->bhi', S, r_t) if emit else None
        return S, y

    S, ys = lax.scan(step, state0, xs, reverse=reverse)
    return S, (jnp.moveaxis(ys, 0, 1) if emit else None)


def rwkv_branch(prep, states0, r_k, lnx_g, lnx_b, gate, emit):
    r, kk, v, decay, k_dir, b_dir = prep
    S_f, y_f = rwkv_scan(states0[0], r, decay[0], k_dir[0], v, -kk, b_dir[0], False, emit)
    S_b, y_b = rwkv_scan(states0[1], r, decay[1], k_dir[1], v, -kk, b_dir[1], True, emit)
    if not emit:
        return None, (S_f, S_b)
    y = y_f + y_b
    mean = jnp.mean(y, -1, keepdims=True)
    var = jnp.mean(jnp.square(y - mean), -1, keepdims=True)
    yn = (y - mean) * lax.rsqrt(var + LNX_EPS) * to_heads(lnx_g).astype(jnp.float32) \
        + to_heads(lnx_b).astype(jnp.float32)
    bonus = jnp.sum(r * (k_dir[0] + k_dir[1]) * r_k, -1, keepdims=True) * v
    out = (yn + bonus.astype(jnp.float32)).astype(gate.dtype)
    B, L = gate.shape[:2]
    return out.reshape(B, L, RW_WIDTH) * jax.nn.silu(gate), (S_f, S_b)


def rot_half(t, cos, sin):
    t1, t2 = t[..., :AXIS_DIM // 2], t[..., AXIS_DIM // 2:]
    return jnp.concatenate([t1 * cos - t2 * sin, t1 * sin + t2 * cos], axis=-1)


def rope_axial(t, cos_r, sin_r, cos_c, sin_c):
    return jnp.concatenate([rot_half(t[..., :AXIS_DIM], cos_r, sin_r),
                            rot_half(t[..., AXIS_DIM:], cos_c, sin_c)], axis=-1)


def mla_queries(u_mla, q_g, w_uq, rot):
    B, L = u_mla.shape[:2]
    c_q = rms_norm(u_mla[..., :Q_LORA], q_g)
    q = jnp.einsum('blr,rf->blf', c_q, w_uq).reshape(B, L, MLA_HEADS, QK_DIM)
    if rot is not None:
        q = jnp.concatenate([q[..., :QK_NOPE], rope_axial(q[..., QK_NOPE:], *rot)], axis=-1)
    return q


def mla_keys_values(u_mla, kv_g, w_ukv, rot):
    B, L = u_mla.shape[:2]
    c_kv = rms_norm(u_mla[..., Q_LORA:Q_LORA + KV_LORA], kv_g)
    k_r = u_mla[..., Q_LORA + KV_LORA:][:, :, None, :]
    if rot is not None:
        k_r = rope_axial(k_r, *rot)
    kv = jnp.einsum('blr,rf->blf', c_kv, w_ukv).reshape(B, L, MLA_HEADS, QK_NOPE + V_HEAD)
    k_nope, v = kv[..., :QK_NOPE], kv[..., QK_NOPE:]
    k = jnp.concatenate([k_nope, jnp.broadcast_to(k_r, (B, L, MLA_HEADS, QK_ROPE))], axis=-1)
    return k, v


def softmax_attend(q, k, v):
    s = jnp.einsum('bqhd,bkhd->bhqk', q, k).astype(jnp.float32) * ATTN_SCALE
    p = jax.nn.softmax(s, axis=-1).astype(v.dtype)
    return jnp.einsum('bhqk,bkhd->bqhd', p, v)


def block_attention(q, k_all, v_all):
    B, L, H, dk = q.shape
    nb = L // Q_BLOCK
    qb = q.reshape(B, nb, Q_BLOCK, H, dk).swapaxes(0, 1)
    o = lax.map(lambda blk: softmax_attend(blk, k_all, v_all), qb)
    return o.swapaxes(0, 1).reshape(B, L, H * V_HEAD)


def hybrid_layer(x, ctx, mod, mod_c, rot, params, need_ctx_out):
    (norm_g, w_in, shift_mu, rw_w0, rw_w2, rw_a0, rw_a2, rw_kk, rw_ka, rw_rk,
     rw_lnx_g, rw_lnx_b, q_g, kv_g, w_uq, w_ukv, w_out) = params
    B, L = x.shape[:2]
    shift, scale, gate = jnp.split(mod, 3, axis=-1)
    shift_c, scale_c, gate_c = jnp.split(mod_c, 3, axis=-1)
    h = rms_norm(x, norm_g) * (1 + scale) + shift
    hc = rms_norm(ctx, norm_g) * (1 + scale_c) + shift_c
    u_sh, g_rw, u_mla, g_mla = jnp.split(h @ w_in, SPLITS, axis=-1)
    uc_sh, gc_rw, uc_mla, gc_mla = jnp.split(hc @ w_in, SPLITS, axis=-1)

    rw_p = (shift_mu, rw_w0, rw_w2, rw_a0, rw_a2, rw_kk, rw_ka)
    S0 = jnp.zeros((B, RW_HEADS, RW_HEAD, RW_HEAD), jnp.float32)
    rw_c, ctx_states = rwkv_branch(rwkv_prep(uc_sh, *rw_p), (S0, S0), rw_rk, rw_lnx_g, rw_lnx_b,
                                   gc_rw, need_ctx_out)
    rw_l, _ = rwkv_branch(rwkv_prep(u_sh, *rw_p), ctx_states, rw_rk, rw_lnx_g, rw_lnx_b, g_rw, True)

    kc, vc = mla_keys_values(uc_mla, kv_g, w_ukv, None)
    k, v = mla_keys_values(u_mla, kv_g, w_ukv, rot)
    q = mla_queries(u_mla, q_g, w_uq, rot)
    o = block_attention(q, jnp.concatenate([k, kc], axis=1), jnp.concatenate([v, vc], axis=1))
    mla_l = o * jax.nn.silu(g_mla)

    x = x + gate * (jnp.concatenate([rw_l, mla_l], axis=-1) @ w_out)
    if need_ctx_out:
        qc = mla_queries(uc_mla, q_g, w_uq, None)
        mla_c = softmax_attend(qc, kc, vc).reshape(B, ctx.shape[1], MLA_WIDTH) * jax.nn.silu(gc_mla)
        ctx = ctx + gate_c * (jnp.concatenate([rw_c, mla_c], axis=-1) @ w_out)
    return x, ctx


def setup_inputs(seed: int = 0) -> dict:
    key = jax.random.key(seed)
    ks = jax.random.split(key, 24)
    f32 = jnp.float32

    def nrm(k, shape, s):
        return jax.random.normal(k, shape, f32) * s

    return {
        'x': nrm(ks[0], (BATCH, SEQ, D_MODEL), 1.0),
        'c': nrm(ks[1], (BATCH, D_MODEL), 1.0),
        'ctx': nrm(ks[2], (BATCH, CTX_LEN, D_MODEL), 1.0),
        'c_ctx': nrm(ks[3], (D_MODEL,), 1.0),
        'ada_w': nrm(ks[4], (DEPTH, D_MODEL, 3 * D_MODEL), 0.5 * D_MODEL ** -0.5),
        'ada_b': nrm(ks[5], (DEPTH, 3 * D_MODEL), 0.02),
        'norm_g': 1.0 + nrm(ks[6], (DEPTH, D_MODEL), 0.05),
        'w_in': nrm(ks[7], (DEPTH, D_MODEL, D_IN), D_MODEL ** -0.5),
        'shift_mu': jax.random.uniform(ks[8], (DEPTH, 2, N_SHIFT), f32, 0.0, 0.5),
        'rw_w0': jax.random.uniform(ks[9], (DEPTH, 2, RW_WIDTH), f32, -6.5, -1.5),
        'rw_w2': nrm(ks[10], (DEPTH, 2, DECAY_LORA, RW_WIDTH), 0.1 * DECAY_LORA ** -0.5),
        'rw_a0': nrm(ks[11], (DEPTH, 2, RW_WIDTH), 0.1),
        'rw_a2': nrm(ks[12], (DEPTH, 2, AAA_LORA, RW_WIDTH), AAA_LORA ** -0.5),
        'rw_kk': 0.85 + nrm(ks[13], (DEPTH, RW_WIDTH), 0.02),
        'rw_ka': 1.0 + nrm(ks[14], (DEPTH, RW_WIDTH), 0.02),
        'rw_rk': nrm(ks[15], (DEPTH, RW_HEADS, RW_HEAD), 0.1),
        'rw_lnx_g': 1.0 + nrm(ks[16], (DEPTH, RW_WIDTH), 0.05),
        'rw_lnx_b': nrm(ks[17], (DEPTH, RW_WIDTH), 0.02),
        'mla_q_norm_g': 1.0 + nrm(ks[18], (DEPTH, Q_LORA), 0.05),
        'mla_kv_norm_g': 1.0 + nrm(ks[19], (DEPTH, KV_LORA), 0.05),
        'mla_w_uq': nrm(ks[20], (DEPTH, Q_LORA, MLA_HEADS * QK_DIM), Q_LORA ** -0.5),
        'mla_w_ukv': nrm(ks[21], (DEPTH, KV_LORA, MLA_HEADS * (QK_NOPE + V_HEAD)), KV_LORA ** -0.5),
        'w_out': nrm(ks[22], (DEPTH, D_MIX, D_MODEL), D_MIX ** -0.5),
        'final_g': 1.0 + nrm(ks[23], (D_MODEL,), 0.05),
    }


def reference(x, c, ctx, c_ctx, ada_w, ada_b, norm_g, w_in, shift_mu, rw_w0, rw_w2, rw_a0, rw_a2,
              rw_kk, rw_ka, rw_rk, rw_lnx_g, rw_lnx_b, mla_q_norm_g, mla_kv_norm_g, mla_w_uq,
              mla_w_ukv, w_out, final_g):
    L = x.shape[1]
    rows = L // GRID_W
    row = jnp.repeat(jnp.arange(rows, dtype=jnp.float32), GRID_W)
    col = jnp.tile(jnp.arange(GRID_W, dtype=jnp.float32), rows)
    inv_freq = ROPE_THETA ** (-jnp.arange(0, AXIS_DIM, 2, dtype=jnp.float32) / AXIS_DIM)
    ang_r = (row[:, None] * inv_freq)[:, None, :]
    ang_c = (col[:, None] * inv_freq)[:, None, :]
    rot = (jnp.cos(ang_r).astype(x.dtype), jnp.sin(ang_r).astype(x.dtype),
           jnp.cos(ang_c).astype(x.dtype), jnp.sin(ang_c).astype(x.dtype))

    for i in range(DEPTH):
        mod = (jax.nn.silu(c) @ ada_w[i] + ada_b[i])[:, None, :]
        mod_c = jax.nn.silu(c_ctx) @ ada_w[i] + ada_b[i]
        params = (norm_g[i], w_in[i], shift_mu[i], rw_w0[i], rw_w2[i], rw_a0[i], rw_a2[i], rw_kk[i],
                  rw_ka[i], rw_rk[i], rw_lnx_g[i], rw_lnx_b[i], mla_q_norm_g[i], mla_kv_norm_g[i],
                  mla_w_uq[i], mla_w_ukv[i], w_out[i])
        x, ctx = hybrid_layer(x, ctx, mod, mod_c, rot, params, i < DEPTH - 1)
    return rms_norm(x, final_g)
```

```python
import functools
import math

import numpy as np
import jax
import jax.numpy as jnp
from jax import lax
from jax.experimental import pallas as pl
from jax.experimental.pallas import tpu as pltpu

F32 = jnp.float32
BF16 = jnp.bfloat16

RW_HEAD = 64
RW_HEADS = 8
RW_WIDTH = RW_HEAD * RW_HEADS
DECAY_LORA = 64
AAA_LORA = 64
MLA_HEADS = 8
QK_NOPE = 64
QK_ROPE = 32
QK_DIM = QK_NOPE + QK_ROPE
V_HEAD = 64
MLA_WIDTH = MLA_HEADS * V_HEAD
Q_LORA = 384
KV_LORA = 256
AXIS_DIM = QK_ROPE // 2
ROPE_THETA = 10000.0
GRID_W = 64
NORM_EPS = 1e-6
LNX_EPS = 64e-5
ATTN_SCALE = QK_DIM ** -0.5
N_SHIFT = 3 * RW_WIDTH + 2 * DECAY_LORA + 2 * AAA_LORA

LANES = 128
QK_PAD = LANES
CHUNK = 64
VMEM_LIMIT = 48 * 1024 * 1024

NN = (((1,), (0,)), ((), ()))
NT = (((1,), (1,)), ((), ()))
TN = (((0,), (0,)), ((), ()))


def _dot(a, b, dims=NN):
    return lax.dot_general(a, b, dims, preferred_element_type=F32)


def _split2(x):
    hi = x.astype(BF16)
    lo = (x - hi.astype(F32)).astype(BF16)
    return hi, lo


def _split3(x):
    hi = x.astype(BF16)
    r1 = x - hi.astype(F32)
    mid = r1.astype(BF16)
    lo = (r1 - mid.astype(F32)).astype(BF16)
    return hi, mid, lo


def _mm(a, b, passes, dims=NN):
    if passes == 1:
        return _dot(a.astype(BF16), b.astype(BF16), dims)
    if passes == 3:
        ah, al = _split2(a)
        bh, bl = _split2(b)
        return _dot(ah, bh, dims) + (_dot(ah, bl, dims) + _dot(al, bh, dims))
    ah, am, al = _split3(a)
    bh, bm, bl = _split3(b)
    return (_dot(ah, bh, dims) + (_dot(ah, bm, dims) + _dot(am, bh, dims))
            + (_dot(ah, bl, dims) + _dot(al, bh, dims) + _dot(am, bm, dims)))


def _mm_exact_rhs(a, b_bf16, nsplit, dims=NN):
    if nsplit == 1:
        return _dot(a.astype(BF16), b_bf16, dims)
    parts = _split2(a) if nsplit == 2 else _split3(a)
    out = _dot(parts[0], b_bf16, dims)
    for p in parts[1:]:
        out = out + _dot(p, b_bf16, dims)
    return out


def _mm_exact_lhs(a_bf16, b, nsplit, dims=NN):
    parts = _split2(b) if nsplit == 2 else _split3(b)
    out = _dot(a_bf16, parts[0], dims)
    for p in parts[1:]:
        out = out + _dot(a_bf16, p, dims)
    return out


def _sigmoid(x):
    return 1.0 / (1.0 + jnp.exp(-x))


def _silu(x):
    return x * _sigmoid(x)


def _params(sem):
    return pltpu.CompilerParams(dimension_semantics=sem, vmem_limit_bytes=VMEM_LIMIT)


def _mod_kernel(c_ref, w_ref, b_ref, o_ref):
    s = _silu(c_ref[...])
    o_ref[...] = _mm(s, w_ref[...], 6) + b_ref[...]


def _mod_call(c_rows, ada_w, ada_b):
    n, d = c_rows.shape
    d3 = ada_w.shape[1]
    tn = 512
    return pl.pallas_call(
        _mod_kernel,
        grid=(d3 // tn,),
        in_specs=[pl.BlockSpec((n, d), lambda j: (0, 0)),
                  pl.BlockSpec((d, tn), lambda j: (0, j)),
                  pl.BlockSpec((1, tn), lambda j: (0, j))],
        out_specs=pl.BlockSpec((n, tn), lambda j: (0, j)),
        out_shape=jax.ShapeDtypeStruct((n, d3), F32),
        compiler_params=_params(("arbitrary",)),
        name="mod",
    )(c_rows, ada_w, ada_b.reshape(1, d3))


def _rope(t, cos, sins, lane):
    reps = t.shape[1] // LANES
    cos_f = jnp.concatenate([cos] * reps, axis=1) if reps > 1 else cos
    sin_f = jnp.concatenate([sins] * reps, axis=1) if reps > 1 else sins
    n = t.shape[1]
    lo_half = ((lane % LANES) % (AXIS_DIM)) < (AXIS_DIM // 2)
    partner = jnp.where(lo_half, pltpu.roll(t, n - AXIS_DIM // 2, axis=1), pltpu.roll(t, AXIS_DIM // 2, axis=1))
    return t * cos_f + partner * sin_f


def _inproj_kernel(*refs, rope, want_q, q_scale):
    if rope:
        (x_ref, mod_ref, ng_ref, w_ref, qg_ref, kvg_ref, wuq_ref, wk_ref, wv_ref, cos_ref, sin_ref), outs = refs[:11], refs[11:]
    else:
        (x_ref, mod_ref, ng_ref, w_ref, qg_ref, kvg_ref, wuq_ref, wk_ref, wv_ref), outs = refs[:9], refs[9:]
    x = x_ref[0]
    d = x.shape[1]
    shift = mod_ref[0, 0:1, :]
    scale = mod_ref[0, 1:2, :]
    y = x * lax.rsqrt(jnp.mean(x * x, axis=-1, keepdims=True) + NORM_EPS) * ng_ref[...]
    h = (y * (1.0 + scale) + shift).astype(BF16)
    u = _dot(h, w_ref[...])
    o0 = 0
    u_sh = u[:, o0:o0 + N_SHIFT]; o0 += N_SHIFT
    g_rw = u[:, o0:o0 + RW_WIDTH]; o0 += RW_WIDTH
    g_mla = u[:, o0:o0 + MLA_WIDTH]; o0 += MLA_WIDTH
    cq = u[:, o0:o0 + Q_LORA]; o0 += Q_LORA
    ckv = u[:, o0:o0 + KV_LORA]; o0 += KV_LORA
    kr = u[:, o0:o0 + LANES]

    ckv_n = (ckv * lax.rsqrt(jnp.mean(ckv * ckv, axis=-1, keepdims=True) + NORM_EPS) * kvg_ref[...]).astype(BF16)
    k = _dot(ckv_n, wk_ref[...])
    v = _dot(ckv_n, wv_ref[...])
    if rope:
        cos = cos_ref[...]
        sins = sin_ref[...]
        lane1 = lax.broadcasted_iota(jnp.int32, kr.shape, 1)
        kr = _rope(kr, cos, sins, lane1)
    k = k + jnp.concatenate([kr] * MLA_HEADS, axis=1)

    if want_q:
        u_ref, grw_ref, gmla_ref, q_ref, k_ref, v_ref = outs
        grw_ref[0] = _silu(g_rw)
        gmla_ref[0] = _silu(g_mla)
        cq_n = (cq * lax.rsqrt(jnp.mean(cq * cq, axis=-1, keepdims=True) + NORM_EPS) * qg_ref[...]).astype(BF16)
        q = _dot(cq_n, wuq_ref[...])
        if rope:
            lane8 = lax.broadcasted_iota(jnp.int32, q.shape, 1)
            q = _rope(q, cos, sins, lane8)
        q_ref[0] = (q * q_scale).astype(BF16)
    else:
        u_ref, k_ref, v_ref = outs
    u_ref[0] = u_sh
    k_ref[0] = k.astype(BF16)
    v_ref[0] = v.astype(BF16)


def _inproj_call(x, mod3, norm_g, w_cat, q_g, kv_g, wuq, wk, wv, cos_t, sin_t, *, rope, want_q, tm):
    b, l, d = x.shape
    nw = w_cat.shape[1]
    const = lambda bb, i: (0, 0)
    in_specs = [pl.BlockSpec((1, tm, d), lambda bb, i: (bb, i, 0)),
                pl.BlockSpec((1, 3, d), lambda bb, i: (bb, 0, 0)),
                pl.BlockSpec((1, d), const),
                pl.BlockSpec((d, nw), const),
                pl.BlockSpec((1, Q_LORA), const),
                pl.BlockSpec((1, KV_LORA), const),
                pl.BlockSpec(wuq.shape, const),
                pl.BlockSpec(wk.shape, const),
                pl.BlockSpec(wv.shape, const)]
    args = [x, mod3, norm_g, w_cat, q_g, kv_g, wuq, wk, wv]
    if rope:
        in_specs += [pl.BlockSpec((tm, LANES), lambda bb, i: (i, 0)),
                     pl.BlockSpec((tm, LANES), lambda bb, i: (i, 0))]
        args += [cos_t, sin_t]
    row = lambda w: pl.BlockSpec((1, tm, w), lambda bb, i: (bb, i, 0))
    sds = lambda w, dt: jax.ShapeDtypeStruct((b, l, w), dt)
    if want_q:
        out_specs = [row(N_SHIFT), row(RW_WIDTH), row(MLA_WIDTH), row(MLA_HEADS * QK_PAD), row(MLA_HEADS * QK_PAD), row(MLA_WIDTH)]
        out_shape = [sds(N_SHIFT, F32), sds(RW_WIDTH, F32), sds(MLA_WIDTH, F32),
                     sds(MLA_HEADS * QK_PAD, BF16), sds(MLA_HEADS * QK_PAD, BF16), sds(MLA_WIDTH, BF16)]
    else:
        out_specs = [row(N_SHIFT), row(MLA_HEADS * QK_PAD), row(MLA_WIDTH)]
        out_shape = [sds(N_SHIFT, F32), sds(MLA_HEADS * QK_PAD, BF16), sds(MLA_WIDTH, BF16)]
    kern = functools.partial(_inproj_kernel, rope=rope, want_q=want_q,
                             q_scale=ATTN_SCALE * math.log2(math.e))
    return pl.pallas_call(
        kern, grid=(b, l // tm), in_specs=in_specs, out_specs=out_specs, out_shape=out_shape,
        compiler_params=_params(("parallel", "arbitrary")),
        name="inproj_lat" if want_q else "inproj_ctx",
    )(*args)


def _prep_kernel(u_ref, up_ref, un_ref, mu_ref, w0_ref, w2_ref, a0_ref, a2_ref, kk_ref, ka_ref, rk_ref, ones_ref,
                 r_ref, v_ref, a_ref, lwf_ref, lwb_ref, kf_ref, kb_ref, bf_ref, bb_ref, bon_ref):
    i = pl.program_id(1)
    n = pl.num_programs(1)
    u = u_ref[0]
    tm = u.shape[0]
    prev_row = jnp.where(i > 0, up_ref[0, 7:8, :], 0.0)
    next_row = jnp.where(i < n - 1, un_ref[0, 0:1, :], 0.0)
    rowi = lax.broadcasted_iota(jnp.int32, u.shape, 0)
    u_prev = jnp.where(rowi == 0, prev_row, pltpu.roll(u, 1, axis=0))
    u_next = jnp.where(rowi == tm - 1, next_row, pltpu.roll(u, tm - 1, axis=0))
    us = u + mu_ref[0:1, :] * (u_prev - u) + mu_ref[1:2, :] * (u_next - u)

    w = RW_WIDTH
    r = us[:, 0:w]
    k = us[:, w:2 * w]
    v = us[:, 2 * w:3 * w]
    w_in = us[:, 3 * w:3 * w + LANES]
    a_in = us[:, 3 * w + LANES:3 * w + 2 * LANES]

    z = _mm(jnp.tanh(w_in), w2_ref[...], 3) + w0_ref[...]
    lw = -math.exp(-0.5) * _sigmoid(z)
    a_sig = _sigmoid(_mm(a_in, a2_ref[...], 3) + a0_ref[...])

    ones = ones_ref[...]
    kkf = k * kk_ref[...]
    ss = _mm_exact_rhs(kkf * kkf, ones, 3)
    kk = kkf * lax.rsqrt(jnp.maximum(ss, 1e-24))
    ka = ka_ref[...]
    k_f = k * (1.0 + (a_sig[:, :w] - 1.0) * ka)
    k_b = k * (1.0 + (a_sig[:, w:] - 1.0) * ka)
    bonus = _mm_exact_rhs(r * (k_f + k_b) * rk_ref[...], ones, 3) * v

    r_ref[0] = r
    v_ref[0] = v
    a_ref[0] = -kk
    lwf_ref[0] = lw[:, :w]
    lwb_ref[0] = lw[:, w:]
    kf_ref[0] = k_f
    kb_ref[0] = k_b
    bf_ref[0] = kk * a_sig[:, :w]
    bb_ref[0] = kk * a_sig[:, w:]
    bon_ref[0] = bonus


def _prep_call(u_sh, mu, w0, w2bd, a0, a2bd, kk, ka, rk, ones_blk, *, tm):
    b, l, ns = u_sh.shape
    const = lambda bb, i: (0, 0)
    nb8 = l // 8
    t8 = tm // 8
    in_specs = [pl.BlockSpec((1, tm, ns), lambda bb, i: (bb, i, 0)),
                pl.BlockSpec((1, 8, ns), lambda bb, i: (bb, jnp.maximum(i * t8 - 1, 0), 0)),
                pl.BlockSpec((1, 8, ns), lambda bb, i: (bb, jnp.minimum((i + 1) * t8, nb8 - 1), 0)),
                pl.BlockSpec(mu.shape, const), pl.BlockSpec(w0.shape, const), pl.BlockSpec(w2bd.shape, const),
                pl.BlockSpec(a0.shape, const), pl.BlockSpec(a2bd.shape, const), pl.BlockSpec(kk.shape, const),
                pl.BlockSpec(ka.shape, const), pl.BlockSpec(rk.shape, const), pl.BlockSpec(ones_blk.shape, const)]
    row = pl.BlockSpec((1, tm, RW_WIDTH), lambda bb, i: (bb, i, 0))
    sds = jax.ShapeDtypeStruct((b, l, RW_WIDTH), F32)
    return pl.pallas_call(
        _prep_kernel, grid=(b, l // tm), in_specs=in_specs, out_specs=[row] * 10, out_shape=[sds] * 10,
        compiler_params=_params(("parallel", "arbitrary")),
        name="prep",
    )(u_sh, u_sh, u_sh, mu, w0, w2bd, a0, a2bd, kk, ka, rk, ones_blk)


def _chunk_terms(r, kd, v, a, bd, lw, rev, consts):
    tri_bf, strict, incl, eye, lane_lo, blkmask, eye_p = consts
    c = r.shape[0]
    g = _mm_exact_lhs(tri_bf, lw, 3)
    gp = g - lw
    g_end = g[0:1, :] if rev else g[c - 1:c, :]
    eg = jnp.exp(g)
    eng = jnp.exp(-g)
    rt = r * eg
    at = a * jnp.exp(gp)
    bt = bd * eng
    kt = kd * eng
    e_end = jnp.exp(g_end - g)
    bh = bd * e_end
    kh = kd * e_end

    rhs = jnp.concatenate([bt, kt], axis=0)
    w_parts, u_parts, rq_parts, y0_parts = [], [], [], []
    for hh in range(2):
        m = lane_lo if hh == 0 else jnp.logical_not(lane_lo)
        lhs = jnp.concatenate([jnp.where(m, at, 0.0), jnp.where(m, rt, 0.0)], axis=0)
        aa = _mm(lhs, rhs, 3, NT)
        a_ab = jnp.where(strict, aa[:c, :c], 0.0)
        a_ak = jnp.where(strict, aa[:c, c:], 0.0)
        a_rb = jnp.where(incl, aa[c:, :c], 0.0)
        a_rk = jnp.where(incl, aa[c:, c:], 0.0)
        x = eye + a_ab
        p = a_ab
        for _ in range(int(math.log2(c)) - 1):
            p = _mm(p, p, 3)
            x = x + _mm(x, p, 3)
        akv = _mm(a_ak, v, 3)
        wm = _mm(x, at, 3)
        u0 = _mm(x, akv, 3)
        rq = rt + _mm(a_rb, wm, 3)
        y0 = _mm(a_rb, u0, 3) + _mm(a_rk, v, 3)
        w_parts.append(wm); u_parts.append(u0); rq_parts.append(rq); y0_parts.append(y0)
    sel = lambda parts: jnp.where(lane_lo, parts[0], parts[1])
    wm, u0, rq, y0 = sel(w_parts), sel(u_parts), sel(rq_parts), sel(y0_parts)
    m_bd = jnp.where(blkmask, _mm(bh, wm, 3, TN), 0.0) + eye_p * jnp.exp(g_end)
    n_bd = jnp.where(blkmask, _mm(bh, u0, 3, TN) + _mm(kh, v, 3, TN), 0.0)
    return rq, y0, m_bd, n_bd


def _scan_kernel(rf_ref, vf_ref, af_ref, lwf_ref, kf_ref, bf_ref,
                 rb_ref, vb_ref, ab_ref, lwb_ref, kb_ref, bb_ref, s0_ref,
                 yf_ref, yb_ref, sout_ref, sf_scr, sb_scr):
    i = pl.program_id(2)
    n = pl.num_programs(2)
    c = rf_ref.shape[1]

    @pl.when(i == 0)
    def _():
        sf_scr[...] = s0_ref[0, 0, 0]
        sb_scr[...] = s0_ref[0, 1, 0]

    row = lax.broadcasted_iota(jnp.int32, (c, c), 0)
    col = lax.broadcasted_iota(jnp.int32, (c, c), 1)
    eye = (row == col).astype(F32)
    lane_lo = lax.broadcasted_iota(jnp.int32, (c, LANES), 1) < RW_HEAD
    prow = lax.broadcasted_iota(jnp.int32, (LANES, LANES), 0)
    pcol = lax.broadcasted_iota(jnp.int32, (LANES, LANES), 1)
    blkmask = (prow // RW_HEAD) == (pcol // RW_HEAD)
    eye_p = (prow == pcol).astype(F32)

    for rev, refs, y_ref, s_scr in ((False, (rf_ref, kf_ref, vf_ref, af_ref, bf_ref, lwf_ref), yf_ref, sf_scr),
                                    (True, (rb_ref, kb_ref, vb_ref, ab_ref, bb_ref, lwb_ref), yb_ref, sb_scr)):
        if rev:
            tri, strict, incl = (col >= row), (col > row), (col >= row)
        else:
            tri, strict, incl = (col <= row), (col < row), (col <= row)
        consts = (tri.astype(BF16), strict, incl, eye, lane_lo, blkmask, eye_p)
        r, kd, v, a, bd, lw = (x[0] for x in refs)
        rq, y0, m_bd, n_bd = _chunk_terms(r, kd, v, a, bd, lw, rev, consts)
        s = s_scr[...]
        y_ref[0] = _mm(rq, s, 3) + y0
        s_scr[...] = _mm(m_bd, s, 3) + n_bd

    @pl.when(i == n - 1)
    def _():
        sout_ref[0, 0, 0] = sf_scr[...]
        sout_ref[0, 1, 0] = sb_scr[...]


def _scan_call(r, v, a, lw_f, lw_b, k_f, k_b, b_f, b_b, s0):
    b, l, w = r.shape
    nc = l // CHUNK
    npair = w // LANES
    fwd = pl.BlockSpec((1, CHUNK, LANES), lambda bb, p, i: (bb, i, p))
    bwd = pl.BlockSpec((1, CHUNK, LANES), lambda bb, p, i: (bb, nc - 1 - i, p))
    st = pl.BlockSpec((1, 2, 1, LANES, LANES), lambda bb, p, i: (bb, 0, p, 0, 0))
    sds = jax.ShapeDtypeStruct((b, l, w), F32)
    return pl.pallas_call(
        _scan_kernel, grid=(b, npair, nc),
        in_specs=[fwd] * 6 + [bwd] * 6 + [st],
        out_specs=[fwd, bwd, st],
        out_shape=[sds, sds, jax.ShapeDtypeStruct(s0.shape, F32)],
        scratch_shapes=[pltpu.VMEM((LANES, LANES), F32), pltpu.VMEM((LANES, LANES), F32)],
        compiler_params=_params(("parallel", "parallel", "arbitrary")),
        name="scan",
    )(r, v, a, lw_f, k_f, b_f, r, v, a, lw_b, k_b, b_b, s0)


def _attn_kernel(q_ref, k_ref, v_ref, o_ref, *, tk):
    tq = q_ref.shape[1]
    lk = k_ref.shape[1]
    nk = lk // tk
    lane_lo = lax.broadcasted_iota(jnp.int32, (tk, LANES), 1) < V_HEAD
    lane_lo_q = lax.broadcasted_iota(jnp.int32, (tq, LANES), 1) < V_HEAD
    q0 = q_ref[0, :, 0:QK_PAD]
    q1 = q_ref[0, :, QK_PAD:2 * QK_PAD]

    def body(j, carry):
        m0, l0, m1, l1, acc = carry
        off = pl.multiple_of(j * tk, tk)
        kt = k_ref[0, pl.ds(off, tk), :]
        vt = v_ref[0, pl.ds(off, tk), :]
        s0 = _dot(q0, kt[:, 0:QK_PAD], NT)
        s1 = _dot(q1, kt[:, QK_PAD:2 * QK_PAD], NT)
        m0n = jnp.maximum(m0, jnp.max(s0, axis=-1, keepdims=True))
        m1n = jnp.maximum(m1, jnp.max(s1, axis=-1, keepdims=True))
        p0 = jnp.exp2(s0 - m0n)
        p1 = jnp.exp2(s1 - m1n)
        al0 = jnp.exp2(m0 - m0n)
        al1 = jnp.exp2(m1 - m1n)
        l0n = al0 * l0 + jnp.sum(p0, axis=-1, keepdims=True)
        l1n = al1 * l1 + jnp.sum(p1, axis=-1, keepdims=True)
        zero = jnp.zeros_like(vt)
        pv = _dot(p0.astype(BF16), jnp.where(lane_lo, vt, zero)) + _dot(p1.astype(BF16), jnp.where(lane_lo, zero, vt))
        acc = jnp.where(lane_lo_q, al0, al1) * acc + pv
        return m0n, l0n, m1n, l1n, acc

    neg = jnp.full((tq, 1), -1e30, F32)
    zer = jnp.zeros((tq, 1), F32)
    m0, l0, m1, l1, acc = lax.fori_loop(0, nk, body, (neg, zer, neg, zer, jnp.zeros((tq, LANES), F32)))
    o_ref[0] = acc / jnp.where(lane_lo_q, l0, l1)


def _attn_call(q, k, v, *, tq, tk):
    b, l, _ = q.shape
    lk = k.shape[1]
    npair = MLA_HEADS // 2
    return pl.pallas_call(
        functools.partial(_attn_kernel, tk=tk),
        grid=(b, npair, l // tq),
        in_specs=[pl.BlockSpec((1, tq, 2 * QK_PAD), lambda bb, p, i: (bb, i, p)),
                  pl.BlockSpec((1, lk, 2 * QK_PAD), lambda bb, p, i: (bb, 0, p)),
                  pl.BlockSpec((1, lk, LANES), lambda bb, p, i: (bb, 0, p))],
        out_specs=pl.BlockSpec((1, tq, LANES), lambda bb, p, i: (bb, i, p)),
        out_shape=jax.ShapeDtypeStruct((b, l, MLA_WIDTH), F32),
        compiler_params=_params(("parallel", "parallel", "arbitrary")),
        name="attn",
    )(q, k, v)


def _outproj_kernel(x_ref, mod_ref, yf_ref, yb_ref, bon_ref, grw_ref, o_ref, gmla_ref, lg_ref, lb_ref, ones_ref,
                    wo_ref, fg_ref, out_ref):
    x = x_ref[0]
    gate = mod_ref[0, 2:3, :]
    ones = ones_ref[...]
    y = yf_ref[0] + yb_ref[0]
    inv_n = 1.0 / RW_HEAD
    mean = _mm_exact_rhs(y, ones, 3) * inv_n
    yc = y - mean
    var = _mm_exact_rhs(yc * yc, ones, 3) * inv_n
    yn = yc * lax.rsqrt(var + LNX_EPS) * lg_ref[...] + lb_ref[...]
    rw = (yn + bon_ref[0]) * grw_ref[0]
    mla = o_ref[0] * gmla_ref[0]
    cat = jnp.concatenate([rw, mla], axis=1).astype(BF16)
    z = x + gate * _dot(cat, wo_ref[...])
    out_ref[0] = z * lax.rsqrt(jnp.mean(z * z, axis=-1, keepdims=True) + NORM_EPS) * fg_ref[...]


def _outproj_call(x, mod3, y_f, y_b, bonus, sg_rw, o, sg_mla, lnx_g, lnx_b, ones_blk, w_out, final_g, *, tm):
    b, l, d = x.shape
    const = lambda bb, i: (0, 0)
    row = lambda w: pl.BlockSpec((1, tm, w), lambda bb, i: (bb, i, 0))
    return pl.pallas_call(
        _outproj_kernel, grid=(b, l // tm),
        in_specs=[row(d), pl.BlockSpec((1, 3, d), lambda bb, i: (bb, 0, 0)),
                  row(RW_WIDTH), row(RW_WIDTH), row(RW_WIDTH), row(RW_WIDTH), row(MLA_WIDTH), row(MLA_WIDTH),
                  pl.BlockSpec(lnx_g.shape, const), pl.BlockSpec(lnx_b.shape, const),
                  pl.BlockSpec(ones_blk.shape, const), pl.BlockSpec(w_out.shape, const),
                  pl.BlockSpec(final_g.shape, const)],
        out_specs=row(d),
        out_shape=jax.ShapeDtypeStruct((b, l, d), F32),
        compiler_params=_params(("parallel", "arbitrary")),
        name="outproj",
    )(x, mod3, y_f, y_b, bonus, sg_rw, o, sg_mla, lnx_g, lnx_b, ones_blk, w_out, final_g)


def _rope_tables(l):
    rows = l // GRID_W
    row = jnp.repeat(jnp.arange(rows, dtype=F32), GRID_W)
    col = jnp.tile(jnp.arange(GRID_W, dtype=F32), rows)
    inv_freq = ROPE_THETA ** (-jnp.arange(0, AXIS_DIM, 2, dtype=F32) / AXIS_DIM)
    ang_r = row[:, None] * inv_freq
    ang_c = col[:, None] * inv_freq
    cr, sr, cc, sc = jnp.cos(ang_r), jnp.sin(ang_r), jnp.cos(ang_c), jnp.sin(ang_c)
    one = jnp.ones((l, QK_NOPE), F32)
    pad1 = jnp.ones((l, QK_PAD - QK_DIM), F32)
    zero = jnp.zeros((l, QK_NOPE), F32)
    pad0 = jnp.zeros((l, QK_PAD - QK_DIM), F32)
    cos_t = jnp.concatenate([one, cr, cr, cc, cc, pad1], axis=1)
    sin_t = jnp.concatenate([zero, -sr, sr, -sc, sc, pad0], axis=1)
    return cos_t, sin_t


def kernel(x, c, ctx, c_ctx, ada_w, ada_b, norm_g, w_in, shift_mu, rw_w0, rw_w2, rw_a0, rw_a2, rw_kk, rw_ka, rw_rk,
           rw_lnx_g, rw_lnx_b, mla_q_norm_g, mla_kv_norm_g, mla_w_uq, mla_w_ukv, w_out, final_g):
    assert x.shape[-1] == 2 * RW_WIDTH and w_in.shape[0] == 1, "single-layer block with d_model = 1024"
    b, l, d = x.shape
    lc = ctx.shape[1]

    w = w_in[0]
    o1 = N_SHIFT + RW_WIDTH
    o2 = o1 + Q_LORA + KV_LORA + QK_ROPE
    w_kr = jnp.zeros((d, LANES), F32).at[:, QK_NOPE:QK_DIM].set(w[:, o1 + Q_LORA + KV_LORA:o2])
    w_cat = jnp.concatenate([w[:, :N_SHIFT], w[:, N_SHIFT:o1], w[:, o2:], w[:, o1:o1 + Q_LORA],
                             w[:, o1 + Q_LORA:o1 + Q_LORA + KV_LORA], w_kr], axis=1).astype(BF16)
    wuq = jnp.pad(mla_w_uq[0].reshape(Q_LORA, MLA_HEADS, QK_DIM), ((0, 0), (0, 0), (0, QK_PAD - QK_DIM)))
    wuq = wuq.reshape(Q_LORA, MLA_HEADS * QK_PAD).astype(BF16)
    wukv = mla_w_ukv[0].reshape(KV_LORA, MLA_HEADS, QK_NOPE + V_HEAD)
    wk = jnp.pad(wukv[:, :, :QK_NOPE], ((0, 0), (0, 0), (0, QK_PAD - QK_NOPE))).reshape(KV_LORA, MLA_HEADS * QK_PAD).astype(BF16)
    wv = wukv[:, :, QK_NOPE:].reshape(KV_LORA, MLA_WIDTH).astype(BF16)
    zl = jnp.zeros((DECAY_LORA, RW_WIDTH), F32)
    w2bd = jnp.concatenate([jnp.concatenate([rw_w2[0, 0], zl], axis=1), jnp.concatenate([zl, rw_w2[0, 1]], axis=1)], axis=0)
    a2bd = jnp.concatenate([jnp.concatenate([rw_a2[0, 0], zl], axis=1), jnp.concatenate([zl, rw_a2[0, 1]], axis=1)], axis=0)
    w0 = rw_w0[0].reshape(1, 2 * RW_WIDTH)
    a0 = rw_a0[0].reshape(1, 2 * RW_WIDTH)
    hid = np.arange(RW_WIDTH) // RW_HEAD
    ones_blk = jnp.asarray(hid[:, None] == hid[None, :], BF16)
    cos_t, sin_t = _rope_tables(l)
    row1 = lambda t: t.reshape(1, -1)

    c_rows = jnp.zeros((8, d), F32).at[:b].set(c).at[b].set(c_ctx)
    mod = _mod_call(c_rows, ada_w[0], ada_b[0])
    mod_lat = mod[:b].reshape(b, 3, d)
    mod_ctx = jnp.broadcast_to(mod[b].reshape(1, 3, d), (b, 3, d))

    proj = functools.partial(_inproj_call, norm_g=row1(norm_g[0]), w_cat=w_cat, q_g=row1(mla_q_norm_g[0]),
                             kv_g=row1(mla_kv_norm_g[0]), wuq=wuq, wk=wk, wv=wv, cos_t=cos_t, sin_t=sin_t)
    u_sh, sg_rw, sg_mla, q, k_lat, v_lat = proj(x, mod_lat, rope=True, want_q=True, tm=min(256, l))
    uc_sh, k_ctx, v_ctx = proj(ctx, mod_ctx, rope=False, want_q=False, tm=min(256, lc))

    prep = functools.partial(_prep_call, mu=shift_mu[0], w0=w0, w2bd=w2bd, a0=a0, a2bd=a2bd, kk=row1(rw_kk[0]),
                             ka=row1(rw_ka[0]), rk=row1(rw_rk[0]), ones_blk=ones_blk)
    pc = prep(uc_sh, tm=min(256, lc))
    pz = prep(u_sh, tm=min(256, l))
    s_zero = jnp.zeros((b, 2, RW_WIDTH // LANES, LANES, LANES), F32)
    _, _, s_ctx = _scan_call(*pc[:9], s_zero)
    y_f, y_b, _ = _scan_call(*pz[:9], s_ctx)

    k_all = jnp.concatenate([k_lat, k_ctx], axis=1)
    v_all = jnp.concatenate([v_lat, v_ctx], axis=1)
    lk = l + lc
    tk = next(t for t in (768, 512, 384, 256, 128) if lk % t == 0)
    o = _attn_call(q, k_all, v_all, tq=min(256, l), tk=tk)

    return _outproj_call(x, mod_lat, y_f, y_b, pz[9], sg_rw, o, sg_mla, row1(rw_lnx_g[0]), row1(rw_lnx_b[0]),
                         ones_blk, w_out[0].astype(BF16), row1(final_g), tm=min(256, l))
```

```python
import functools
import math

import numpy as np
import jax
import jax.numpy as jnp
from jax import lax
from jax.experimental import pallas as pl
from jax.experimental.pallas import tpu as pltpu

F32 = jnp.float32
BF16 = jnp.bfloat16

RW_HEAD = 64
RW_HEADS = 8
RW_WIDTH = RW_HEAD * RW_HEADS
DECAY_LORA = 64
AAA_LORA = 64
MLA_HEADS = 8
QK_NOPE = 64
QK_ROPE = 32
QK_DIM = QK_NOPE + QK_ROPE
V_HEAD = 64
MLA_WIDTH = MLA_HEADS * V_HEAD
Q_LORA = 384
KV_LORA = 256
AXIS_DIM = QK_ROPE // 2
ROPE_THETA = 10000.0
GRID_W = 64
NORM_EPS = 1e-6
LNX_EPS = 64e-5
ATTN_SCALE = QK_DIM ** -0.5
N_SHIFT = 3 * RW_WIDTH + 2 * DECAY_LORA + 2 * AAA_LORA

LANES = 128
QK_PAD = LANES
CHUNK = 64
VMEM_LIMIT = 48 * 1024 * 1024

NN = (((1,), (0,)), ((), ()))
NT = (((1,), (1,)), ((), ()))
TN = (((0,), (0,)), ((), ()))


def _dot(a, b, dims=NN):
    return lax.dot_general(a, b, dims, preferred_element_type=F32)


def _split2(x):
    hi = x.astype(BF16)
    lo = (x - hi.astype(F32)).astype(BF16)
    return hi, lo


def _split3(x):
    hi = x.astype(BF16)
    r1 = x - hi.astype(F32)
    mid = r1.astype(BF16)
    lo = (r1 - mid.astype(F32)).astype(BF16)
    return hi, mid, lo


def _mm(a, b, passes, dims=NN):
    if passes == 1:
        return _dot(a.astype(BF16), b.astype(BF16), dims)
    if passes == 3:
        ah, al = _split2(a)
        bh, bl = _split2(b)
        return _dot(ah, bh, dims) + (_dot(ah, bl, dims) + _dot(al, bh, dims))
    ah, am, al = _split3(a)
    bh, bm, bl = _split3(b)
    return (_dot(ah, bh, dims) + (_dot(ah, bm, dims) + _dot(am, bh, dims))
            + (_dot(ah, bl, dims) + _dot(al, bh, dims) + _dot(am, bm, dims)))


def _mm_exact_rhs(a, b_bf16, nsplit, dims=NN):
    if nsplit == 1:
        return _dot(a.astype(BF16), b_bf16, dims)
    parts = _split2(a) if nsplit == 2 else _split3(a)
    out = _dot(parts[0], b_bf16, dims)
    for p in parts[1:]:
        out = out + _dot(p, b_bf16, dims)
    return out


def _mm_exact_lhs(a_bf16, b, nsplit, dims=NN):
    parts = _split2(b) if nsplit == 2 else _split3(b)
    out = _dot(a_bf16, parts[0], dims)
    for p in parts[1:]:
        out = out + _dot(a_bf16, p, dims)
    return out


def _sigmoid(x):
    return 1.0 / (1.0 + jnp.exp(-x))


def _silu(x):
    return x * _sigmoid(x)


def _params(sem):
    return pltpu.CompilerParams(dimension_semantics=sem, vmem_limit_bytes=VMEM_LIMIT)


def _mod_kernel(c_ref, w_ref, b_ref, o_ref):
    s = _silu(c_ref[...])
    o_ref[...] = _mm(s, w_ref[...], 6) + b_ref[...]


def _mod_call(c_rows, ada_w, ada_b):
    n, d = c_rows.shape
    d3 = ada_w.shape[1]
    tn = 512
    return pl.pallas_call(
        _mod_kernel,
        grid=(d3 // tn,),
        in_specs=[pl.BlockSpec((n, d), lambda j: (0, 0)),
                  pl.BlockSpec((d, tn), lambda j: (0, j)),
                  pl.BlockSpec((1, tn), lambda j: (0, j))],
        out_specs=pl.BlockSpec((n, tn), lambda j: (0, j)),
        out_shape=jax.ShapeDtypeStruct((n, d3), F32),
        compiler_params=_params(("arbitrary",)),
        name="mod",
    )(c_rows, ada_w, ada_b.reshape(1, d3))


def _rope(t, cos, sins, lane):
    reps = t.shape[1] // LANES
    cos_f = jnp.concatenate([cos] * reps, axis=1) if reps > 1 else cos
    sin_f = jnp.concatenate([sins] * reps, axis=1) if reps > 1 else sins
    n = t.shape[1]
    lo_half = ((lane % LANES) % (AXIS_DIM)) < (AXIS_DIM // 2)
    partner = jnp.where(lo_half, pltpu.roll(t, n - AXIS_DIM // 2, axis=1), pltpu.roll(t, AXIS_DIM // 2, axis=1))
    return t * cos_f + partner * sin_f


def _inproj_kernel(*refs, rope, want_q, q_scale):
    if rope:
        (x_ref, mod_ref, ng_ref, w_ref, qg_ref, kvg_ref, wuq_ref, wk_ref, wv_ref, cos_ref, sin_ref), outs = refs[:11], refs[11:]
    else:
        (x_ref, mod_ref, ng_ref, w_ref, qg_ref, kvg_ref, wuq_ref, wk_ref, wv_ref), outs = refs[:9], refs[9:]
    x = x_ref[0]
    d = x.shape[1]
    shift = mod_ref[0, 0:1, :]
    scale = mod_ref[0, 1:2, :]
    y = x * lax.rsqrt(jnp.mean(x * x, axis=-1, keepdims=True) + NORM_EPS) * ng_ref[...]
    h = (y * (1.0 + scale) + shift).astype(BF16)
    u = _dot(h, w_ref[...])
    o0 = 0
    u_sh = u[:, o0:o0 + N_SHIFT]; o0 += N_SHIFT
    g_rw = u[:, o0:o0 + RW_WIDTH]; o0 += RW_WIDTH
    g_mla = u[:, o0:o0 + MLA_WIDTH]; o0 += MLA_WIDTH
    cq = u[:, o0:o0 + Q_LORA]; o0 += Q_LORA
    ckv = u[:, o0:o0 + KV_LORA]; o0 += KV_LORA
    kr = u[:, o0:o0 + LANES]

    ckv_n = (ckv * lax.rsqrt(jnp.mean(ckv * ckv, axis=-1, keepdims=True) + NORM_EPS) * kvg_ref[...]).astype(BF16)
    k = _dot(ckv_n, wk_ref[...])
    v = _dot(ckv_n, wv_ref[...])
    if rope:
        cos = cos_ref[...]
        sins = sin_ref[...]
        lane1 = lax.broadcasted_iota(jnp.int32, kr.shape, 1)
        kr = _rope(kr, cos, sins, lane1)
    k = k + jnp.concatenate([kr] * MLA_HEADS, axis=1)

    if want_q:
        u_ref, grw_ref, gmla_ref, q_ref, k_ref, v_ref = outs
        grw_ref[0] = _silu(g_rw)
        gmla_ref[0] = _silu(g_mla)
        cq_n = (cq * lax.rsqrt(jnp.mean(cq * cq, axis=-1, keepdims=True) + NORM_EPS) * qg_ref[...]).astype(BF16)
        q = _dot(cq_n, wuq_ref[...])
        if rope:
            lane8 = lax.broadcasted_iota(jnp.int32, q.shape, 1)
            q = _rope(q, cos, sins, lane8)
        q_ref[0] = (q * q_scale).astype(BF16)
    else:
        u_ref, k_ref, v_ref = outs
    u_ref[0] = u_sh
    k_ref[0] = k.astype(BF16)
    v_ref[0] = v.astype(BF16)


def _inproj_call(x, mod3, norm_g, w_cat, q_g, kv_g, wuq, wk, wv, cos_t, sin_t, *, rope, want_q, tm):
    b, l, d = x.shape
    nw = w_cat.shape[1]
    const = lambda bb, i: (0, 0)
    in_specs = [pl.BlockSpec((1, tm, d), lambda bb, i: (bb, i, 0)),
                pl.BlockSpec((1, 3, d), lambda bb, i: (bb, 0, 0)),
                pl.BlockSpec((1, d), const),
                pl.BlockSpec((d, nw), const),
                pl.BlockSpec((1, Q_LORA), const),
                pl.BlockSpec((1, KV_LORA), const),
                pl.BlockSpec(wuq.shape, const),
                pl.BlockSpec(wk.shape, const),
                pl.BlockSpec(wv.shape, const)]
    args = [x, mod3, norm_g, w_cat, q_g, kv_g, wuq, wk, wv]
    if rope:
        in_specs += [pl.BlockSpec((tm, LANES), lambda bb, i: (i, 0)),
                     pl.BlockSpec((tm, LANES), lambda bb, i: (i, 0))]
        args += [cos_t, sin_t]
    row = lambda w: pl.BlockSpec((1, tm, w), lambda bb, i: (bb, i, 0))
    sds = lambda w, dt: jax.ShapeDtypeStruct((b, l, w), dt)
    if want_q:
        out_specs = [row(N_SHIFT), row(RW_WIDTH), row(MLA_WIDTH), row(MLA_HEADS * QK_PAD), row(MLA_HEADS * QK_PAD), row(MLA_WIDTH)]
        out_shape = [sds(N_SHIFT, F32), sds(RW_WIDTH, F32), sds(MLA_WIDTH, F32),
                     sds(MLA_HEADS * QK_PAD, BF16), sds(MLA_HEADS * QK_PAD, BF16), sds(MLA_WIDTH, BF16)]
    else:
        out_specs = [row(N_SHIFT), row(MLA_HEADS * QK_PAD), row(MLA_WIDTH)]
        out_shape = [sds(N_SHIFT, F32), sds(MLA_HEADS * QK_PAD, BF16), sds(MLA_WIDTH, BF16)]
    kern = functools.partial(_inproj_kernel, rope=rope, want_q=want_q,
                             q_scale=ATTN_SCALE * math.log2(math.e))
    return pl.pallas_call(
        kern, grid=(b, l // tm), in_specs=in_specs, out_specs=out_specs, out_shape=out_shape,
        compiler_params=_params(("parallel", "arbitrary")),
        name="inproj_lat" if want_q else "inproj_ctx",
    )(*args)


def _prep_kernel(u_ref, up_ref, un_ref, mu_ref, w0_ref, w2_ref, a0_ref, a2_ref, kk_ref, ka_ref, rk_ref, ones_ref,
                 r_ref, v_ref, a_ref, lwf_ref, lwb_ref, kf_ref, kb_ref, bf_ref, bb_ref, bon_ref):
    i = pl.program_id(1)
    n = pl.num_programs(1)
    u = u_ref[0]
    tm = u.shape[0]
    prev_row = jnp.where(i > 0, up_ref[0, 7:8, :], 0.0)
    next_row = jnp.where(i < n - 1, un_ref[0, 0:1, :], 0.0)
    rowi = lax.broadcasted_iota(jnp.int32, u.shape, 0)
    u_prev = jnp.where(rowi == 0, prev_row, pltpu.roll(u, 1, axis=0))
    u_next = jnp.where(rowi == tm - 1, next_row, pltpu.roll(u, tm - 1, axis=0))
    us = u + mu_ref[0:1, :] * (u_prev - u) + mu_ref[1:2, :] * (u_next - u)

    w = RW_WIDTH
    r = us[:, 0:w]
    k = us[:, w:2 * w]
    v = us[:, 2 * w:3 * w]
    w_in = us[:, 3 * w:3 * w + LANES]
    a_in = us[:, 3 * w + LANES:3 * w + 2 * LANES]

    z = _mm(jnp.tanh(w_in), w2_ref[...], 3) + w0_ref[...]
    lw = -math.exp(-0.5) * _sigmoid(z)
    a_sig = _sigmoid(_mm(a_in, a2_ref[...], 3) + a0_ref[...])

    ones = ones_ref[...]
    kkf = k * kk_ref[...]
    ss = _mm_exact_rhs(kkf * kkf, ones, 3)
    kk = kkf * lax.rsqrt(jnp.maximum(ss, 1e-24))
    ka = ka_ref[...]
    k_f = k * (1.0 + (a_sig[:, :w] - 1.0) * ka)
    k_b = k * (1.0 + (a_sig[:, w:] - 1.0) * ka)
    bonus = _mm_exact_rhs(r * (k_f + k_b) * rk_ref[...], ones, 3) * v

    r_ref[0] = r
    v_ref[0] = v
    a_ref[0] = -kk
    lwf_ref[0] = lw[:, :w]
    lwb_ref[0] = lw[:, w:]
    kf_ref[0] = k_f
    kb_ref[0] = k_b
    bf_ref[0] = kk * a_sig[:, :w]
    bb_ref[0] = kk * a_sig[:, w:]
    bon_ref[0] = bonus


def _prep_call(u_sh, mu, w0, w2bd, a0, a2bd, kk, ka, rk, ones_blk, *, tm):
    b, l, ns = u_sh.shape
    const = lambda bb, i: (0, 0)
    nb8 = l // 8
    t8 = tm // 8
    in_specs = [pl.BlockSpec((1, tm, ns), lambda bb, i: (bb, i, 0)),
                pl.BlockSpec((1, 8, ns), lambda bb, i: (bb, jnp.maximum(i * t8 - 1, 0), 0)),
                pl.BlockSpec((1, 8, ns), lambda bb, i: (bb, jnp.minimum((i + 1) * t8, nb8 - 1), 0)),
                pl.BlockSpec(mu.shape, const), pl.BlockSpec(w0.shape, const), pl.BlockSpec(w2bd.shape, const),
                pl.BlockSpec(a0.shape, const), pl.BlockSpec(a2bd.shape, const), pl.BlockSpec(kk.shape, const),
                pl.BlockSpec(ka.shape, const), pl.BlockSpec(rk.shape, const), pl.BlockSpec(ones_blk.shape, const)]
    row = pl.BlockSpec((1, tm, RW_WIDTH), lambda bb, i: (bb, i, 0))
    sds = jax.ShapeDtypeStruct((b, l, RW_WIDTH), F32)
    return pl.pallas_call(
        _prep_kernel, grid=(b, l // tm), in_specs=in_specs, out_specs=[row] * 10, out_shape=[sds] * 10,
        compiler_params=_params(("parallel", "arbitrary")),
        name="prep",
    )(u_sh, u_sh, u_sh, mu, w0, w2bd, a0, a2bd, kk, ka, rk, ones_blk)


QUAD = 4 * RW_HEAD


def _stack_heads(x, masks):
    zero = jnp.zeros_like(x)
    return jnp.concatenate([jnp.where(m, x, zero) for m in masks], axis=0)


def _unstack_heads(x, c):
    return (x[0:c] + x[c:2 * c]) + (x[2 * c:3 * c] + x[3 * c:4 * c])


def _scan_chain(r, kd, v, a, bd, lw, sv, rev, consts):
    tri_bf, strict, incl, eye, masks, prow, pcol = consts
    c = r.shape[0]
    g = _mm_exact_lhs(tri_bf, lw, 3)
    g_end = g[0:1, :] if rev else g[c - 1:c, :]
    eng = jnp.exp(-g)
    e_end = jnp.exp(g_end - g)
    rt = r * jnp.exp(g)
    stk = lambda t: _stack_heads(t.astype(BF16), masks)
    at_s = stk(a * jnp.exp(g - lw))
    rt_s = stk(rt)
    bt_s = stk(bd * eng)
    kt_s = stk(kd * eng)
    bh_s = stk(bd * e_end)
    kh_s = stk(kd * e_end)
    v_s = stk(v)

    a_ab = jnp.where(strict, _dot(at_s, bt_s, NT), 0.0)
    a_ak = jnp.where(strict, _dot(at_s, kt_s, NT), 0.0).astype(BF16)
    a_rb = jnp.where(incl, _dot(rt_s, bt_s, NT), 0.0).astype(BF16)
    a_rk = jnp.where(incl, _dot(rt_s, kt_s, NT), 0.0).astype(BF16)
    blk = lambda sz: (prow // sz) == (pcol // sz)
    lb = a_ab.astype(BF16)
    zero = jnp.zeros_like(lb)
    t_inv = eye + jnp.where(blk(2), a_ab, 0.0)
    sz = 2
    while sz < c:
        off = jnp.where(jnp.logical_and(blk(2 * sz), jnp.logical_not(blk(sz))), lb, zero)
        tb = t_inv.astype(BF16)
        t_inv = t_inv + _dot(tb, _dot(off, tb).astype(BF16))
        sz *= 2
    tb = t_inv.astype(BF16)
    akv = _dot(a_ak, v_s).astype(BF16)
    wu = _dot(tb, jnp.concatenate([at_s, akv], axis=1)).astype(BF16)
    ry = _dot(a_rb, wu)
    rq = (rt + _unstack_heads(ry[:, :QUAD], c)).astype(BF16)
    y0 = _unstack_heads(ry[:, QUAD:] + _dot(a_rk, v_s), c)
    sv_b = sv.astype(BF16)
    y = _dot(rq, sv_b, NT) + y0
    sw = _dot(sv_b, wu[:, :QUAD], NT).astype(BF16)
    sv_new = (sv * jnp.exp(g_end) + _dot(sw, bh_s)
              + _dot(jnp.concatenate([wu[:, QUAD:], v_s], axis=0), jnp.concatenate([bh_s, kh_s], axis=0), TN))
    return y, sv_new


def _scan_kernel(rf_ref, vf_ref, af_ref, lwf_ref, kf_ref, bf_ref,
                 rb_ref, vb_ref, ab_ref, lwb_ref, kb_ref, bb_ref, s0_ref,
                 yf_ref, yb_ref, sout_ref, s_scr):
    i = pl.program_id(1)
    n = pl.num_programs(1)
    c = rf_ref.shape[1]
    nquad = rf_ref.shape[2] // QUAD

    @pl.when(i == 0)
    def _():
        s_scr[...] = s0_ref[0]

    row = lax.broadcasted_iota(jnp.int32, (c, c), 0)
    col = lax.broadcasted_iota(jnp.int32, (c, c), 1)
    prow = lax.broadcasted_iota(jnp.int32, (QUAD, QUAD), 0)
    pcol = lax.broadcasted_iota(jnp.int32, (QUAD, QUAD), 1)
    eye = (prow == pcol).astype(F32)
    lane = lax.broadcasted_iota(jnp.int32, (c, QUAD), 1)
    masks = [(lane // RW_HEAD) == j for j in range(QUAD // RW_HEAD)]

    for d, rev, refs, y_ref in ((0, False, (rf_ref, kf_ref, vf_ref, af_ref, bf_ref, lwf_ref), yf_ref),
                                (1, True, (rb_ref, kb_ref, vb_ref, ab_ref, bb_ref, lwb_ref), yb_ref)):
        if rev:
            tri, strict, incl = (col >= row), (pcol > prow), (pcol >= prow)
        else:
            tri, strict, incl = (col <= row), (pcol < prow), (pcol <= prow)
        consts = (tri.astype(BF16), strict, incl, eye, masks, prow, pcol)
        for q in range(nquad):
            ins = (x[0, :, q * QUAD:(q + 1) * QUAD] for x in refs)
            y, sv_new = _scan_chain(*ins, s_scr[d, q], rev, consts)
            y_ref[0, :, q * QUAD:(q + 1) * QUAD] = y
            s_scr[d, q] = sv_new

    @pl.when(i == n - 1)
    def _():
        sout_ref[0] = s_scr[...]


def _scan_call(r, v, a, lw_f, lw_b, k_f, k_b, b_f, b_b, s0):
    b, l, w = r.shape
    nc = l // CHUNK
    fwd = pl.BlockSpec((1, CHUNK, w), lambda bb, i: (bb, i, 0))
    bwd = pl.BlockSpec((1, CHUNK, w), lambda bb, i: (bb, nc - 1 - i, 0))
    st = pl.BlockSpec((1,) + s0.shape[1:], lambda bb, i: (bb, 0, 0, 0, 0))
    sds = jax.ShapeDtypeStruct((b, l, w), F32)
    return pl.pallas_call(
        _scan_kernel, grid=(b, nc),
        in_specs=[fwd] * 6 + [bwd] * 6 + [st],
        out_specs=[fwd, bwd, st],
        out_shape=[sds, sds, jax.ShapeDtypeStruct(s0.shape, F32)],
        scratch_shapes=[pltpu.VMEM(s0.shape[1:], F32)],
        compiler_params=_params(("parallel", "arbitrary")),
        name="scan",
    )(r, v, a, lw_f, k_f, b_f, r, v, a, lw_b, k_b, b_b, s0)


def _attn_kernel(q_ref, k_ref, v_ref, o_ref, *, tk):
    tq = q_ref.shape[1]
    lk = k_ref.shape[1]
    nk = lk // tk
    lane_lo = lax.broadcasted_iota(jnp.int32, (tk, LANES), 1) < V_HEAD
    lane_lo_q = lax.broadcasted_iota(jnp.int32, (tq, LANES), 1) < V_HEAD
    q0 = q_ref[0, :, 0:QK_PAD]
    q1 = q_ref[0, :, QK_PAD:2 * QK_PAD]

    def body(j, carry):
        m0, l0, m1, l1, acc = carry
        off = pl.multiple_of(j * tk, tk)
        kt = k_ref[0, pl.ds(off, tk), :]
        vt = v_ref[0, pl.ds(off, tk), :]
        s0 = _dot(q0, kt[:, 0:QK_PAD], NT)
        s1 = _dot(q1, kt[:, QK_PAD:2 * QK_PAD], NT)
        m0n = jnp.maximum(m0, jnp.max(s0, axis=-1, keepdims=True))
        m1n = jnp.maximum(m1, jnp.max(s1, axis=-1, keepdims=True))
        p0 = jnp.exp2(s0 - m0n)
        p1 = jnp.exp2(s1 - m1n)
        al0 = jnp.exp2(m0 - m0n)
        al1 = jnp.exp2(m1 - m1n)
        l0n = al0 * l0 + jnp.sum(p0, axis=-1, keepdims=True)
        l1n = al1 * l1 + jnp.sum(p1, axis=-1, keepdims=True)
        zero = jnp.zeros_like(vt)
        pv = _dot(p0.astype(BF16), jnp.where(lane_lo, vt, zero)) + _dot(p1.astype(BF16), jnp.where(lane_lo, zero, vt))
        acc = jnp.where(lane_lo_q, al0, al1) * acc + pv
        return m0n, l0n, m1n, l1n, acc

    neg = jnp.full((tq, 1), -1e30, F32)
    zer = jnp.zeros((tq, 1), F32)
    m0, l0, m1, l1, acc = lax.fori_loop(0, nk, body, (neg, zer, neg, zer, jnp.zeros((tq, LANES), F32)))
    o_ref[0] = acc / jnp.where(lane_lo_q, l0, l1)


def _attn_call(q, k, v, *, tq, tk):
    b, l, _ = q.shape
    lk = k.shape[1]
    npair = MLA_HEADS // 2
    return pl.pallas_call(
        functools.partial(_attn_kernel, tk=tk),
        grid=(b, npair, l // tq),
        in_specs=[pl.BlockSpec((1, tq, 2 * QK_PAD), lambda bb, p, i: (bb, i, p)),
                  pl.BlockSpec((1, lk, 2 * QK_PAD), lambda bb, p, i: (bb, 0, p)),
                  pl.BlockSpec((1, lk, LANES), lambda bb, p, i: (bb, 0, p))],
        out_specs=pl.BlockSpec((1, tq, LANES), lambda bb, p, i: (bb, i, p)),
        out_shape=jax.ShapeDtypeStruct((b, l, MLA_WIDTH), F32),
        compiler_params=_params(("parallel", "parallel", "arbitrary")),
        name="attn",
    )(q, k, v)


def _outproj_kernel(x_ref, mod_ref, yf_ref, yb_ref, bon_ref, grw_ref, o_ref, gmla_ref, lg_ref, lb_ref, ones_ref,
                    wo_ref, fg_ref, out_ref):
    x = x_ref[0]
    gate = mod_ref[0, 2:3, :]
    ones = ones_ref[...]
    y = yf_ref[0] + yb_ref[0]
    inv_n = 1.0 / RW_HEAD
    mean = _mm_exact_rhs(y, ones, 3) * inv_n
    yc = y - mean
    var = _mm_exact_rhs(yc * yc, ones, 3) * inv_n
    yn = yc * lax.rsqrt(var + LNX_EPS) * lg_ref[...] + lb_ref[...]
    rw = (yn + bon_ref[0]) * grw_ref[0]
    mla = o_ref[0] * gmla_ref[0]
    cat = jnp.concatenate([rw, mla], axis=1).astype(BF16)
    z = x + gate * _dot(cat, wo_ref[...])
    out_ref[0] = z * lax.rsqrt(jnp.mean(z * z, axis=-1, keepdims=True) + NORM_EPS) * fg_ref[...]


def _outproj_call(x, mod3, y_f, y_b, bonus, sg_rw, o, sg_mla, lnx_g, lnx_b, ones_blk, w_out, final_g, *, tm):
    b, l, d = x.shape
    const = lambda bb, i: (0, 0)
    row = lambda w: pl.BlockSpec((1, tm, w), lambda bb, i: (bb, i, 0))
    return pl.pallas_call(
        _outproj_kernel, grid=(b, l // tm),
        in_specs=[row(d), pl.BlockSpec((1, 3, d), lambda bb, i: (bb, 0, 0)),
                  row(RW_WIDTH), row(RW_WIDTH), row(RW_WIDTH), row(RW_WIDTH), row(MLA_WIDTH), row(MLA_WIDTH),
                  pl.BlockSpec(lnx_g.shape, const), pl.BlockSpec(lnx_b.shape, const),
                  pl.BlockSpec(ones_blk.shape, const), pl.BlockSpec(w_out.shape, const),
                  pl.BlockSpec(final_g.shape, const)],
        out_specs=row(d),
        out_shape=jax.ShapeDtypeStruct((b, l, d), F32),
        compiler_params=_params(("parallel", "arbitrary")),
        name="outproj",
    )(x, mod3, y_f, y_b, bonus, sg_rw, o, sg_mla, lnx_g, lnx_b, ones_blk, w_out, final_g)


def _rope_tables(l):
    rows = l // GRID_W
    row = jnp.repeat(jnp.arange(rows, dtype=F32), GRID_W)
    col = jnp.tile(jnp.arange(GRID_W, dtype=F32), rows)
    inv_freq = ROPE_THETA ** (-jnp.arange(0, AXIS_DIM, 2, dtype=F32) / AXIS_DIM)
    ang_r = row[:, None] * inv_freq
    ang_c = col[:, None] * inv_freq
    cr, sr, cc, sc = jnp.cos(ang_r), jnp.sin(ang_r), jnp.cos(ang_c), jnp.sin(ang_c)
    one = jnp.ones((l, QK_NOPE), F32)
    pad1 = jnp.ones((l, QK_PAD - QK_DIM), F32)
    zero = jnp.zeros((l, QK_NOPE), F32)
    pad0 = jnp.zeros((l, QK_PAD - QK_DIM), F32)
    cos_t = jnp.concatenate([one, cr, cr, cc, cc, pad1], axis=1)
    sin_t = jnp.concatenate([zero, -sr, sr, -sc, sc, pad0], axis=1)
    return cos_t, sin_t


def kernel(x, c, ctx, c_ctx, ada_w, ada_b, norm_g, w_in, shift_mu, rw_w0, rw_w2, rw_a0, rw_a2, rw_kk, rw_ka, rw_rk,
           rw_lnx_g, rw_lnx_b, mla_q_norm_g, mla_kv_norm_g, mla_w_uq, mla_w_ukv, w_out, final_g):
    assert x.shape[-1] == 2 * RW_WIDTH and w_in.shape[0] == 1, "single-layer block with d_model = 1024"
    b, l, d = x.shape
    lc = ctx.shape[1]

    w = w_in[0]
    o1 = N_SHIFT + RW_WIDTH
    o2 = o1 + Q_LORA + KV_LORA + QK_ROPE
    w_kr = jnp.zeros((d, LANES), F32).at[:, QK_NOPE:QK_DIM].set(w[:, o1 + Q_LORA + KV_LORA:o2])
    w_cat = jnp.concatenate([w[:, :N_SHIFT], w[:, N_SHIFT:o1], w[:, o2:], w[:, o1:o1 + Q_LORA],
                             w[:, o1 + Q_LORA:o1 + Q_LORA + KV_LORA], w_kr], axis=1).astype(BF16)
    wuq = jnp.pad(mla_w_uq[0].reshape(Q_LORA, MLA_HEADS, QK_DIM), ((0, 0), (0, 0), (0, QK_PAD - QK_DIM)))
    wuq = wuq.reshape(Q_LORA, MLA_HEADS * QK_PAD).astype(BF16)
    wukv = mla_w_ukv[0].reshape(KV_LORA, MLA_HEADS, QK_NOPE + V_HEAD)
    wk = jnp.pad(wukv[:, :, :QK_NOPE], ((0, 0), (0, 0), (0, QK_PAD - QK_NOPE))).reshape(KV_LORA, MLA_HEADS * QK_PAD).astype(BF16)
    wv = wukv[:, :, QK_NOPE:].reshape(KV_LORA, MLA_WIDTH).astype(BF16)
    zl = jnp.zeros((DECAY_LORA, RW_WIDTH), F32)
    w2bd = jnp.concatenate([jnp.concatenate([rw_w2[0, 0], zl], axis=1), jnp.concatenate([zl, rw_w2[0, 1]], axis=1)], axis=0)
    a2bd = jnp.concatenate([jnp.concatenate([rw_a2[0, 0], zl], axis=1), jnp.concatenate([zl, rw_a2[0, 1]], axis=1)], axis=0)
    w0 = rw_w0[0].reshape(1, 2 * RW_WIDTH)
    a0 = rw_a0[0].reshape(1, 2 * RW_WIDTH)
    hid = np.arange(RW_WIDTH) // RW_HEAD
    ones_blk = jnp.asarray(hid[:, None] == hid[None, :], BF16)
    cos_t, sin_t = _rope_tables(l)
    row1 = lambda t: t.reshape(1, -1)

    c_rows = jnp.zeros((8, d), F32).at[:b].set(c).at[b].set(c_ctx)
    mod = _mod_call(c_rows, ada_w[0], ada_b[0])
    mod_lat = mod[:b].reshape(b, 3, d)
    mod_ctx = jnp.broadcast_to(mod[b].reshape(1, 3, d), (b, 3, d))

    proj = functools.partial(_inproj_call, norm_g=row1(norm_g[0]), w_cat=w_cat, q_g=row1(mla_q_norm_g[0]),
                             kv_g=row1(mla_kv_norm_g[0]), wuq=wuq, wk=wk, wv=wv, cos_t=cos_t, sin_t=sin_t)
    u_sh, sg_rw, sg_mla, q, k_lat, v_lat = proj(x, mod_lat, rope=True, want_q=True, tm=min(256, l))
    uc_sh, k_ctx, v_ctx = proj(ctx, mod_ctx, rope=False, want_q=False, tm=min(256, lc))

    prep = functools.partial(_prep_call, mu=shift_mu[0], w0=w0, w2bd=w2bd, a0=a0, a2bd=a2bd, kk=row1(rw_kk[0]),
                             ka=row1(rw_ka[0]), rk=row1(rw_rk[0]), ones_blk=ones_blk)
    pc = prep(uc_sh, tm=min(256, lc))
    pz = prep(u_sh, tm=min(256, l))
    s_zero = jnp.zeros((b, 2, RW_WIDTH // QUAD, QUAD, QUAD), F32)
    _, _, s_ctx = _scan_call(*pc[:9], s_zero)
    y_f, y_b, _ = _scan_call(*pz[:9], s_ctx)

    k_all = jnp.concatenate([k_lat, k_ctx], axis=1)
    v_all = jnp.concatenate([v_lat, v_ctx], axis=1)
    lk = l + lc
    tk = next(t for t in (768, 512, 384, 256, 128) if lk % t == 0)
    o = _attn_call(q, k_all, v_all, tq=min(256, l), tk=tk)

    return _outproj_call(x, mod_lat, y_f, y_b, pz[9], sg_rw, o, sg_mla, row1(rw_lnx_g[0]), row1(rw_lnx_b[0]),
                         ones_blk, w_out[0].astype(BF16), row1(final_g), tm=min(256, l))
```

```python
import functools
import math

import numpy as np
import jax
import jax.numpy as jnp
from jax import lax
from jax.experimental import pallas as pl
from jax.experimental.pallas import tpu as pltpu

F32 = jnp.float32
BF16 = jnp.bfloat16

RW_HEAD = 64
RW_HEADS = 8
RW_WIDTH = RW_HEAD * RW_HEADS
DECAY_LORA = 64
AAA_LORA = 64
MLA_HEADS = 8
QK_NOPE = 64
QK_ROPE = 32
QK_DIM = QK_NOPE + QK_ROPE
V_HEAD = 64
MLA_WIDTH = MLA_HEADS * V_HEAD
Q_LORA = 384
KV_LORA = 256
AXIS_DIM = QK_ROPE // 2
ROPE_THETA = 10000.0
GRID_W = 64
NORM_EPS = 1e-6
LNX_EPS = 64e-5
ATTN_SCALE = QK_DIM ** -0.5
N_SHIFT = 3 * RW_WIDTH + 2 * DECAY_LORA + 2 * AAA_LORA

LANES = 128
QK_PAD = LANES
CHUNK = 64
VMEM_LIMIT = 48 * 1024 * 1024

NN = (((1,), (0,)), ((), ()))
NT = (((1,), (1,)), ((), ()))
TN = (((0,), (0,)), ((), ()))


def _dot(a, b, dims=NN):
    return lax.dot_general(a, b, dims, preferred_element_type=F32)


def _split2(x):
    hi = x.astype(BF16)
    lo = (x - hi.astype(F32)).astype(BF16)
    return hi, lo


def _split3(x):
    hi = x.astype(BF16)
    r1 = x - hi.astype(F32)
    mid = r1.astype(BF16)
    lo = (r1 - mid.astype(F32)).astype(BF16)
    return hi, mid, lo


def _mm(a, b, passes, dims=NN):
    if passes == 1:
        return _dot(a.astype(BF16), b.astype(BF16), dims)
    if passes == 3:
        ah, al = _split2(a)
        bh, bl = _split2(b)
        return _dot(ah, bh, dims) + (_dot(ah, bl, dims) + _dot(al, bh, dims))
    ah, am, al = _split3(a)
    bh, bm, bl = _split3(b)
    return (_dot(ah, bh, dims) + (_dot(ah, bm, dims) + _dot(am, bh, dims))
            + (_dot(ah, bl, dims) + _dot(al, bh, dims) + _dot(am, bm, dims)))


def _mm_exact_rhs(a, b_bf16, nsplit, dims=NN):
    if nsplit == 1:
        return _dot(a.astype(BF16), b_bf16, dims)
    parts = _split2(a) if nsplit == 2 else _split3(a)
    out = _dot(parts[0], b_bf16, dims)
    for p in parts[1:]:
        out = out + _dot(p, b_bf16, dims)
    return out


def _mm_exact_lhs(a_bf16, b, nsplit, dims=NN):
    parts = _split2(b) if nsplit == 2 else _split3(b)
    out = _dot(a_bf16, parts[0], dims)
    for p in parts[1:]:
        out = out + _dot(a_bf16, p, dims)
    return out


def _sigmoid(x):
    return 1.0 / (1.0 + jnp.exp(-x))


def _silu(x):
    return x * _sigmoid(x)


def _params(sem):
    return pltpu.CompilerParams(dimension_semantics=sem, vmem_limit_bytes=VMEM_LIMIT)


def _mod_kernel(c_ref, w_ref, b_ref, o_ref):
    s = _silu(c_ref[...])
    o_ref[...] = _mm(s, w_ref[...], 6) + b_ref[...]


def _mod_call(c_rows, ada_w, ada_b):
    n, d = c_rows.shape
    d3 = ada_w.shape[1]
    tn = 512
    return pl.pallas_call(
        _mod_kernel,
        grid=(d3 // tn,),
        in_specs=[pl.BlockSpec((n, d), lambda j: (0, 0)),
                  pl.BlockSpec((d, tn), lambda j: (0, j)),
                  pl.BlockSpec((1, tn), lambda j: (0, j))],
        out_specs=pl.BlockSpec((n, tn), lambda j: (0, j)),
        out_shape=jax.ShapeDtypeStruct((n, d3), F32),
        compiler_params=_params(("arbitrary",)),
        name="mod",
    )(c_rows, ada_w, ada_b.reshape(1, d3))


def _rope(t, cos, sins, lane):
    reps = t.shape[1] // LANES
    cos_f = jnp.concatenate([cos] * reps, axis=1) if reps > 1 else cos
    sin_f = jnp.concatenate([sins] * reps, axis=1) if reps > 1 else sins
    n = t.shape[1]
    lo_half = ((lane % LANES) % (AXIS_DIM)) < (AXIS_DIM // 2)
    partner = jnp.where(lo_half, pltpu.roll(t, n - AXIS_DIM // 2, axis=1), pltpu.roll(t, AXIS_DIM // 2, axis=1))
    return t * cos_f + partner * sin_f


def _inproj_kernel(*refs, rope, want_q, q_scale):
    if rope:
        (x_ref, mod_ref, ng_ref, w_ref, qg_ref, kvg_ref, wuq_ref, wk_ref, wv_ref, cos_ref, sin_ref), outs = refs[:11], refs[11:]
    else:
        (x_ref, mod_ref, ng_ref, w_ref, qg_ref, kvg_ref, wuq_ref, wk_ref, wv_ref), outs = refs[:9], refs[9:]
    x = x_ref[0]
    d = x.shape[1]
    shift = mod_ref[0, 0:1, :]
    scale = mod_ref[0, 1:2, :]
    y = x * lax.rsqrt(jnp.mean(x * x, axis=-1, keepdims=True) + NORM_EPS) * ng_ref[...]
    h = (y * (1.0 + scale) + shift).astype(BF16)
    u = _dot(h, w_ref[...])
    o0 = 0
    u_sh = u[:, o0:o0 + N_SHIFT]; o0 += N_SHIFT
    g_rw = u[:, o0:o0 + RW_WIDTH]; o0 += RW_WIDTH
    g_mla = u[:, o0:o0 + MLA_WIDTH]; o0 += MLA_WIDTH
    cq = u[:, o0:o0 + Q_LORA]; o0 += Q_LORA
    ckv = u[:, o0:o0 + KV_LORA]; o0 += KV_LORA
    kr = u[:, o0:o0 + LANES]

    ckv_n = (ckv * lax.rsqrt(jnp.mean(ckv * ckv, axis=-1, keepdims=True) + NORM_EPS) * kvg_ref[...]).astype(BF16)
    k = _dot(ckv_n, wk_ref[...])
    vt = _dot(wv_ref[...], ckv_n, NT)
    if rope:
        cos = cos_ref[...]
        sins = sin_ref[...]
        lane1 = lax.broadcasted_iota(jnp.int32, kr.shape, 1)
        kr = _rope(kr, cos, sins, lane1)
    k = k + jnp.concatenate([kr] * MLA_HEADS, axis=1)

    if want_q:
        u_ref, grw_ref, gmla_ref, q_ref, k_ref, v_ref = outs
        grw_ref[0] = _silu(g_rw)
        gmla_ref[0] = _silu(g_mla)
        cq_n = (cq * lax.rsqrt(jnp.mean(cq * cq, axis=-1, keepdims=True) + NORM_EPS) * qg_ref[...]).astype(BF16)
        q = _dot(cq_n, wuq_ref[...])
        if rope:
            lane8 = lax.broadcasted_iota(jnp.int32, q.shape, 1)
            q = _rope(q, cos, sins, lane8)
        q_ref[0] = (q * q_scale).astype(BF16)
    else:
        u_ref, k_ref, v_ref = outs
    u_ref[0] = u_sh
    k_ref[0] = k.astype(BF16)
    v_ref[0, 0] = vt.astype(BF16)


def _inproj_call(x, mod3, norm_g, w_cat, q_g, kv_g, wuq, wk, wv, cos_t, sin_t, *, rope, want_q, tm):
    b, l, d = x.shape
    nw = w_cat.shape[1]
    const = lambda bb, i: (0, 0)
    in_specs = [pl.BlockSpec((1, tm, d), lambda bb, i: (bb, i, 0)),
                pl.BlockSpec((1, 3, d), lambda bb, i: (bb, 0, 0)),
                pl.BlockSpec((1, d), const),
                pl.BlockSpec((d, nw), const),
                pl.BlockSpec((1, Q_LORA), const),
                pl.BlockSpec((1, KV_LORA), const),
                pl.BlockSpec(wuq.shape, const),
                pl.BlockSpec(wk.shape, const),
                pl.BlockSpec(wv.shape, const)]
    args = [x, mod3, norm_g, w_cat, q_g, kv_g, wuq, wk, wv]
    if rope:
        in_specs += [pl.BlockSpec((tm, LANES), lambda bb, i: (i, 0)),
                     pl.BlockSpec((tm, LANES), lambda bb, i: (i, 0))]
        args += [cos_t, sin_t]
    row = lambda w: pl.BlockSpec((1, tm, w), lambda bb, i: (bb, i, 0))
    sds = lambda w, dt: jax.ShapeDtypeStruct((b, l, w), dt)
    vt_spec = pl.BlockSpec((1, 1, MLA_WIDTH, tm), lambda bb, i: (bb, i, 0, 0))
    vt_sds = jax.ShapeDtypeStruct((b, l // tm, MLA_WIDTH, tm), BF16)
    if want_q:
        out_specs = [row(N_SHIFT), row(RW_WIDTH), row(MLA_WIDTH), row(MLA_HEADS * QK_PAD), row(MLA_HEADS * QK_PAD), vt_spec]
        out_shape = [sds(N_SHIFT, F32), sds(RW_WIDTH, F32), sds(MLA_WIDTH, F32),
                     sds(MLA_HEADS * QK_PAD, BF16), sds(MLA_HEADS * QK_PAD, BF16), vt_sds]
    else:
        out_specs = [row(N_SHIFT), row(MLA_HEADS * QK_PAD), vt_spec]
        out_shape = [sds(N_SHIFT, F32), sds(MLA_HEADS * QK_PAD, BF16), vt_sds]
    kern = functools.partial(_inproj_kernel, rope=rope, want_q=want_q,
                             q_scale=ATTN_SCALE * math.log2(math.e))
    return pl.pallas_call(
        kern, grid=(b, l // tm), in_specs=in_specs, out_specs=out_specs, out_shape=out_shape,
        compiler_params=_params(("parallel", "arbitrary")),
        name="inproj_lat" if want_q else "inproj_ctx",
    )(*args)


def _prep_kernel(u_ref, up_ref, un_ref, mu_ref, w0_ref, w2_ref, a0_ref, a2_ref, kk_ref, ka_ref, rk_ref, ones_ref,
                 r_ref, v_ref, a_ref, lwf_ref, lwb_ref, kf_ref, kb_ref, bf_ref, bb_ref, bon_ref):
    i = pl.program_id(1)
    n = pl.num_programs(1)
    u = u_ref[0]
    tm = u.shape[0]
    prev_row = jnp.where(i > 0, up_ref[0, 7:8, :], 0.0)
    next_row = jnp.where(i < n - 1, un_ref[0, 0:1, :], 0.0)
    rowi = lax.broadcasted_iota(jnp.int32, u.shape, 0)
    u_prev = jnp.where(rowi == 0, prev_row, pltpu.roll(u, 1, axis=0))
    u_next = jnp.where(rowi == tm - 1, next_row, pltpu.roll(u, tm - 1, axis=0))
    us = u + mu_ref[0:1, :] * (u_prev - u) + mu_ref[1:2, :] * (u_next - u)

    w = RW_WIDTH
    r = us[:, 0:w]
    k = us[:, w:2 * w]
    v = us[:, 2 * w:3 * w]
    w_in = us[:, 3 * w:3 * w + LANES]
    a_in = us[:, 3 * w + LANES:3 * w + 2 * LANES]

    z = _mm(jnp.tanh(w_in), w2_ref[...], 3) + w0_ref[...]
    lw = -math.exp(-0.5) * _sigmoid(z)
    a_sig = _sigmoid(_mm(a_in, a2_ref[...], 3) + a0_ref[...])

    ones = ones_ref[...]
    kkf = k * kk_ref[...]
    ss = _mm_exact_rhs(kkf * kkf, ones, 3)
    kk = kkf * lax.rsqrt(jnp.maximum(ss, 1e-24))
    ka = ka_ref[...]
    k_f = k * (1.0 + (a_sig[:, :w] - 1.0) * ka)
    k_b = k * (1.0 + (a_sig[:, w:] - 1.0) * ka)
    bonus = _mm_exact_rhs(r * (k_f + k_b) * rk_ref[...], ones, 3) * v

    r_ref[0] = r
    v_ref[0] = v
    a_ref[0] = -kk
    lwf_ref[0] = lw[:, :w]
    lwb_ref[0] = lw[:, w:]
    kf_ref[0] = k_f
    kb_ref[0] = k_b
    bf_ref[0] = kk * a_sig[:, :w]
    bb_ref[0] = kk * a_sig[:, w:]
    bon_ref[0] = bonus


def _prep_call(u_sh, mu, w0, w2bd, a0, a2bd, kk, ka, rk, ones_blk, *, tm):
    b, l, ns = u_sh.shape
    const = lambda bb, i: (0, 0)
    nb8 = l // 8
    t8 = tm // 8
    in_specs = [pl.BlockSpec((1, tm, ns), lambda bb, i: (bb, i, 0)),
                pl.BlockSpec((1, 8, ns), lambda bb, i: (bb, jnp.maximum(i * t8 - 1, 0), 0)),
                pl.BlockSpec((1, 8, ns), lambda bb, i: (bb, jnp.minimum((i + 1) * t8, nb8 - 1), 0)),
                pl.BlockSpec(mu.shape, const), pl.BlockSpec(w0.shape, const), pl.BlockSpec(w2bd.shape, const),
                pl.BlockSpec(a0.shape, const), pl.BlockSpec(a2bd.shape, const), pl.BlockSpec(kk.shape, const),
                pl.BlockSpec(ka.shape, const), pl.BlockSpec(rk.shape, const), pl.BlockSpec(ones_blk.shape, const)]
    row = pl.BlockSpec((1, tm, RW_WIDTH), lambda bb, i: (bb, i, 0))
    sds = jax.ShapeDtypeStruct((b, l, RW_WIDTH), F32)
    return pl.pallas_call(
        _prep_kernel, grid=(b, l // tm), in_specs=in_specs, out_specs=[row] * 10, out_shape=[sds] * 10,
        compiler_params=_params(("parallel", "arbitrary")),
        name="prep",
    )(u_sh, u_sh, u_sh, mu, w0, w2bd, a0, a2bd, kk, ka, rk, ones_blk)


QUAD = 4 * RW_HEAD
_DONE = object()


def _stack_heads(x, masks):
    zero = jnp.zeros_like(x)
    return jnp.concatenate([jnp.where(m, x, zero) for m in masks], axis=0)


def _unstack_heads(x, c):
    return (x[0:c] + x[c:2 * c]) + (x[2 * c:3 * c] + x[3 * c:4 * c])


def _scan_chain(r, kd, v, a, bd, lw, sv_ref, y_ref, rev, consts):
    tri_bf, strict, incl, eye, masks, prow, pcol = consts
    c = r.shape[0]
    g = _mm_exact_lhs(tri_bf, lw, 3)
    g_end = g[0:1, :] if rev else g[c - 1:c, :]
    eng = jnp.exp(-g)
    e_end = jnp.exp(g_end - g)
    rt = r * jnp.exp(g)
    stk = lambda t: _stack_heads(t.astype(BF16), masks)
    at_s = stk(a * jnp.exp(g - lw))
    rt_s = stk(rt)
    bt_s = stk(bd * eng)
    kt_s = stk(kd * eng)
    bh_s = stk(bd * e_end)
    kh_s = stk(kd * e_end)
    v_s = stk(v)

    yield
    a_ab = jnp.where(strict, _dot(at_s, bt_s, NT), 0.0)
    yield
    a_ak = jnp.where(strict, _dot(at_s, kt_s, NT), 0.0).astype(BF16)
    yield
    a_rb = jnp.where(incl, _dot(rt_s, bt_s, NT), 0.0).astype(BF16)
    yield
    a_rk = jnp.where(incl, _dot(rt_s, kt_s, NT), 0.0).astype(BF16)
    blk = lambda sz: (prow // sz) == (pcol // sz)
    lb = a_ab.astype(BF16)
    zero = jnp.zeros_like(lb)
    t_inv = eye + jnp.where(blk(2), a_ab, 0.0)
    sz = 2
    while sz < c:
        off = jnp.where(jnp.logical_and(blk(2 * sz), jnp.logical_not(blk(sz))), lb, zero)
        tb = t_inv.astype(BF16)
        yield
        lt = _dot(off, tb).astype(BF16)
        yield
        t_inv = t_inv + _dot(tb, lt)
        sz *= 2
    tb = t_inv.astype(BF16)
    yield
    akv = _dot(a_ak, v_s).astype(BF16)
    yield
    wu = _dot(tb, jnp.concatenate([at_s, akv], axis=1)).astype(BF16)
    yield
    ry = _dot(a_rb, wu)
    rq = (rt + _unstack_heads(ry[:, :QUAD], c)).astype(BF16)
    yield
    y0 = _unstack_heads(ry[:, QUAD:] + _dot(a_rk, v_s), c)
    sv = sv_ref[...]
    sv_b = sv.astype(BF16)
    yield
    y_ref[...] = _dot(rq, sv_b, NT) + y0
    yield
    sw = _dot(sv_b, wu[:, :QUAD], NT).astype(BF16)
    yield
    upd = _dot(jnp.concatenate([wu[:, QUAD:], v_s], axis=0), jnp.concatenate([bh_s, kh_s], axis=0), TN)
    yield
    sv_ref[...] = sv * jnp.exp(g_end) + _dot(sw, bh_s) + upd


def _scan_kernel(rf_ref, vf_ref, af_ref, lwf_ref, kf_ref, bf_ref,
                 rb_ref, vb_ref, ab_ref, lwb_ref, kb_ref, bb_ref, s0_ref,
                 yf_ref, yb_ref, sout_ref, s_scr):
    i = pl.program_id(0)
    n = pl.num_programs(0)
    nb, c = rf_ref.shape[0], rf_ref.shape[1]
    nquad = rf_ref.shape[2] // QUAD

    @pl.when(i == 0)
    def _():
        s_scr[...] = s0_ref[...]

    row = lax.broadcasted_iota(jnp.int32, (c, c), 0)
    col = lax.broadcasted_iota(jnp.int32, (c, c), 1)
    prow = lax.broadcasted_iota(jnp.int32, (QUAD, QUAD), 0)
    pcol = lax.broadcasted_iota(jnp.int32, (QUAD, QUAD), 1)
    eye = (prow == pcol).astype(F32)
    lane = lax.broadcasted_iota(jnp.int32, (c, QUAD), 1)
    masks = [(lane // RW_HEAD) == j for j in range(QUAD // RW_HEAD)]

    chains = []
    for d, rev, refs, y_ref in ((0, False, (rf_ref, kf_ref, vf_ref, af_ref, bf_ref, lwf_ref), yf_ref),
                                (1, True, (rb_ref, kb_ref, vb_ref, ab_ref, bb_ref, lwb_ref), yb_ref)):
        if rev:
            tri, strict, incl = (col >= row), (pcol > prow), (pcol >= prow)
        else:
            tri, strict, incl = (col <= row), (pcol < prow), (pcol <= prow)
        consts = (tri.astype(BF16), strict, incl, eye, masks, prow, pcol)
        for bi in range(nb):
            for q in range(nquad):
                ins = (x[bi, :, q * QUAD:(q + 1) * QUAD] for x in refs)
                chains.append(_scan_chain(*ins, s_scr.at[bi, d, q], y_ref.at[bi, :, q * QUAD:(q + 1) * QUAD],
                                          rev, consts))
    while chains:
        for ch in list(chains):
            if next(ch, _DONE) is _DONE:
                chains.remove(ch)

    @pl.when(i == n - 1)
    def _():
        sout_ref[...] = s_scr[...]


def _scan_call(r, v, a, lw_f, lw_b, k_f, k_b, b_f, b_b, s0):
    b, l, w = r.shape
    nc = l // CHUNK
    fwd = pl.BlockSpec((b, CHUNK, w), lambda i: (0, i, 0))
    bwd = pl.BlockSpec((b, CHUNK, w), lambda i: (0, nc - 1 - i, 0))
    st = pl.BlockSpec(s0.shape, lambda i: (0, 0, 0, 0, 0))
    sds = jax.ShapeDtypeStruct((b, l, w), F32)
    return pl.pallas_call(
        _scan_kernel, grid=(nc,),
        in_specs=[fwd] * 6 + [bwd] * 6 + [st],
        out_specs=[fwd, bwd, st],
        out_shape=[sds, sds, jax.ShapeDtypeStruct(s0.shape, F32)],
        scratch_shapes=[pltpu.VMEM(s0.shape, F32)],
        compiler_params=_params(("arbitrary",)),
        name="scan",
    )(r, v, a, lw_f, k_f, b_f, r, v, a, lw_b, k_b, b_b, s0)


ONES_ROWS = 16


def _attn_kernel(q_ref, k_ref, vt_ref, o_ref, sa_scr, sb_scr, *, r):
    tq = q_ref.shape[1]
    tv = vt_ref.shape[3]
    tk = r * tv
    nk = k_ref.shape[1] // tk
    qs = (q_ref[0, :, 0:QK_PAD], q_ref[0, :, QK_PAD:2 * QK_PAD])
    ones = jnp.ones((ONES_ROWS, tk), BF16)

    def produce(j, s_scr):
        off = pl.multiple_of(j * tk, tk)
        kt = k_ref[0, pl.ds(off, tk), :]
        cmax = []
        for h in range(2):
            s = _dot(kt[:, h * QK_PAD:(h + 1) * QK_PAD], qs[h], NT)
            s_scr[h] = s
            cmax.append(jnp.max(s, axis=0, keepdims=True))
        return tuple(cmax)

    def consume(j, s_scr, cmax, m, acc):
        new_m, new_acc = [], []
        for h in range(2):
            mn = jnp.maximum(m[h], cmax[h])
            p = jnp.exp2(s_scr[h] - mn).astype(BF16)
            alpha = jnp.exp2(m[h] - mn)
            vt = [vt_ref[0, j * r + t, h * V_HEAD:(h + 1) * V_HEAD, :] for t in range(r)]
            vaug = jnp.concatenate([jnp.concatenate(vt, axis=1) if r > 1 else vt[0], ones], axis=0)
            new_acc.append(alpha * acc[h] + _dot(vaug, p))
            new_m.append(mn)
        return tuple(new_m), tuple(new_acc)

    def pair(t, carry):
        cm_a, m, acc = carry
        cm_b = produce(2 * t + 1, sb_scr)
        m, acc = consume(2 * t, sa_scr, cm_a, m, acc)
        cm_a = produce(2 * t + 2, sa_scr)
        m, acc = consume(2 * t + 1, sb_scr, cm_b, m, acc)
        return cm_a, m, acc

    neg = jnp.full((1, tq), -1e30, F32)
    zacc = jnp.zeros((V_HEAD + ONES_ROWS, tq), F32)
    npairs = (nk - 1) // 2
    cm_a, m, acc = lax.fori_loop(0, npairs, pair, (produce(0, sa_scr), (neg, neg), (zacc, zacc)))
    if nk - 1 - 2 * npairs == 1:
        cm_b = produce(nk - 1, sb_scr)
        m, acc = consume(nk - 2, sa_scr, cm_a, m, acc)
        m, acc = consume(nk - 1, sb_scr, cm_b, m, acc)
    else:
        m, acc = consume(nk - 1, sa_scr, cm_a, m, acc)
    ot = jnp.concatenate([acc[h][0:V_HEAD] / acc[h][V_HEAD:V_HEAD + 1] for h in range(2)], axis=0)
    o_ref[0] = ot.T


def _attn_call(q, k, vt, *, tq, r):
    b, l, _ = q.shape
    lk = k.shape[1]
    nt, _, tv = vt.shape[1:]
    npair = MLA_HEADS // 2
    return pl.pallas_call(
        functools.partial(_attn_kernel, r=r),
        grid=(b, npair, l // tq),
        in_specs=[pl.BlockSpec((1, tq, 2 * QK_PAD), lambda bb, p, i: (bb, i, p)),
                  pl.BlockSpec((1, lk, 2 * QK_PAD), lambda bb, p, i: (bb, 0, p)),
                  pl.BlockSpec((1, nt, LANES, tv), lambda bb, p, i: (bb, 0, p, 0))],
        out_specs=pl.BlockSpec((1, tq, LANES), lambda bb, p, i: (bb, i, p)),
        out_shape=jax.ShapeDtypeStruct((b, l, MLA_WIDTH), F32),
        scratch_shapes=[pltpu.VMEM((2, r * tv, tq), F32), pltpu.VMEM((2, r * tv, tq), F32)],
        compiler_params=_params(("parallel", "parallel", "arbitrary")),
        name="attn",
    )(q, k, vt)


def _outproj_kernel(x_ref, mod_ref, yf_ref, yb_ref, bon_ref, grw_ref, o_ref, gmla_ref, lg_ref, lb_ref, ones_ref,
                    wo_ref, fg_ref, out_ref):
    x = x_ref[0]
    gate = mod_ref[0, 2:3, :]
    ones = ones_ref[...]
    y = yf_ref[0] + yb_ref[0]
    inv_n = 1.0 / RW_HEAD
    mean = _mm_exact_rhs(y, ones, 3) * inv_n
    yc = y - mean
    var = _mm_exact_rhs(yc * yc, ones, 3) * inv_n
    yn = yc * lax.rsqrt(var + LNX_EPS) * lg_ref[...] + lb_ref[...]
    rw = (yn + bon_ref[0]) * grw_ref[0]
    mla = o_ref[0] * gmla_ref[0]
    cat = jnp.concatenate([rw, mla], axis=1).astype(BF16)
    z = x + gate * _dot(cat, wo_ref[...])
    out_ref[0] = z * lax.rsqrt(jnp.mean(z * z, axis=-1, keepdims=True) + NORM_EPS) * fg_ref[...]


def _outproj_call(x, mod3, y_f, y_b, bonus, sg_rw, o, sg_mla, lnx_g, lnx_b, ones_blk, w_out, final_g, *, tm):
    b, l, d = x.shape
    const = lambda bb, i: (0, 0)
    row = lambda w: pl.BlockSpec((1, tm, w), lambda bb, i: (bb, i, 0))
    return pl.pallas_call(
        _outproj_kernel, grid=(b, l // tm),
        in_specs=[row(d), pl.BlockSpec((1, 3, d), lambda bb, i: (bb, 0, 0)),
                  row(RW_WIDTH), row(RW_WIDTH), row(RW_WIDTH), row(RW_WIDTH), row(MLA_WIDTH), row(MLA_WIDTH),
                  pl.BlockSpec(lnx_g.shape, const), pl.BlockSpec(lnx_b.shape, const),
                  pl.BlockSpec(ones_blk.shape, const), pl.BlockSpec(w_out.shape, const),
                  pl.BlockSpec(final_g.shape, const)],
        out_specs=row(d),
        out_shape=jax.ShapeDtypeStruct((b, l, d), F32),
        compiler_params=_params(("parallel", "arbitrary")),
        name="outproj",
    )(x, mod3, y_f, y_b, bonus, sg_rw, o, sg_mla, lnx_g, lnx_b, ones_blk, w_out, final_g)


def _rope_tables(l):
    rows = l // GRID_W
    row = jnp.repeat(jnp.arange(rows, dtype=F32), GRID_W)
    col = jnp.tile(jnp.arange(GRID_W, dtype=F32), rows)
    inv_freq = ROPE_THETA ** (-jnp.arange(0, AXIS_DIM, 2, dtype=F32) / AXIS_DIM)
    ang_r = row[:, None] * inv_freq
    ang_c = col[:, None] * inv_freq
    cr, sr, cc, sc = jnp.cos(ang_r), jnp.sin(ang_r), jnp.cos(ang_c), jnp.sin(ang_c)
    one = jnp.ones((l, QK_NOPE), F32)
    pad1 = jnp.ones((l, QK_PAD - QK_DIM), F32)
    zero = jnp.zeros((l, QK_NOPE), F32)
    pad0 = jnp.zeros((l, QK_PAD - QK_DIM), F32)
    cos_t = jnp.concatenate([one, cr, cr, cc, cc, pad1], axis=1)
    sin_t = jnp.concatenate([zero, -sr, sr, -sc, sc, pad0], axis=1)
    return cos_t, sin_t


def kernel(x, c, ctx, c_ctx, ada_w, ada_b, norm_g, w_in, shift_mu, rw_w0, rw_w2, rw_a0, rw_a2, rw_kk, rw_ka, rw_rk,
           rw_lnx_g, rw_lnx_b, mla_q_norm_g, mla_kv_norm_g, mla_w_uq, mla_w_ukv, w_out, final_g):
    assert x.shape[-1] == 2 * RW_WIDTH and w_in.shape[0] == 1, "single-layer block with d_model = 1024"
    b, l, d = x.shape
    lc = ctx.shape[1]

    w = w_in[0]
    o1 = N_SHIFT + RW_WIDTH
    o2 = o1 + Q_LORA + KV_LORA + QK_ROPE
    w_kr = jnp.zeros((d, LANES), F32).at[:, QK_NOPE:QK_DIM].set(w[:, o1 + Q_LORA + KV_LORA:o2])
    w_cat = jnp.concatenate([w[:, :N_SHIFT], w[:, N_SHIFT:o1], w[:, o2:], w[:, o1:o1 + Q_LORA],
                             w[:, o1 + Q_LORA:o1 + Q_LORA + KV_LORA], w_kr], axis=1).astype(BF16)
    wuq = jnp.pad(mla_w_uq[0].reshape(Q_LORA, MLA_HEADS, QK_DIM), ((0, 0), (0, 0), (0, QK_PAD - QK_DIM)))
    wuq = wuq.reshape(Q_LORA, MLA_HEADS * QK_PAD).astype(BF16)
    wukv = mla_w_ukv[0].reshape(KV_LORA, MLA_HEADS, QK_NOPE + V_HEAD)
    wk = jnp.pad(wukv[:, :, :QK_NOPE], ((0, 0), (0, 0), (0, QK_PAD - QK_NOPE))).reshape(KV_LORA, MLA_HEADS * QK_PAD).astype(BF16)
    wv = wukv[:, :, QK_NOPE:].reshape(KV_LORA, MLA_WIDTH).T.astype(BF16)
    zl = jnp.zeros((DECAY_LORA, RW_WIDTH), F32)
    w2bd = jnp.concatenate([jnp.concatenate([rw_w2[0, 0], zl], axis=1), jnp.concatenate([zl, rw_w2[0, 1]], axis=1)], axis=0)
    a2bd = jnp.concatenate([jnp.concatenate([rw_a2[0, 0], zl], axis=1), jnp.concatenate([zl, rw_a2[0, 1]], axis=1)], axis=0)
    w0 = rw_w0[0].reshape(1, 2 * RW_WIDTH)
    a0 = rw_a0[0].reshape(1, 2 * RW_WIDTH)
    hid = np.arange(RW_WIDTH) // RW_HEAD
    ones_blk = jnp.asarray(hid[:, None] == hid[None, :], BF16)
    cos_t, sin_t = _rope_tables(l)
    row1 = lambda t: t.reshape(1, -1)

    c_rows = jnp.zeros((8, d), F32).at[:b].set(c).at[b].set(c_ctx)
    mod = _mod_call(c_rows, ada_w[0], ada_b[0])
    mod_lat = mod[:b].reshape(b, 3, d)
    mod_ctx = jnp.broadcast_to(mod[b].reshape(1, 3, d), (b, 3, d))

    proj = functools.partial(_inproj_call, norm_g=row1(norm_g[0]), w_cat=w_cat, q_g=row1(mla_q_norm_g[0]),
                             kv_g=row1(mla_kv_norm_g[0]), wuq=wuq, wk=wk, wv=wv, cos_t=cos_t, sin_t=sin_t)
    tm_proj = min(256, l, lc)
    u_sh, sg_rw, sg_mla, q, k_lat, vt_lat = proj(x, mod_lat, rope=True, want_q=True, tm=tm_proj)
    uc_sh, k_ctx, vt_ctx = proj(ctx, mod_ctx, rope=False, want_q=False, tm=tm_proj)

    prep = functools.partial(_prep_call, mu=shift_mu[0], w0=w0, w2bd=w2bd, a0=a0, a2bd=a2bd, kk=row1(rw_kk[0]),
                             ka=row1(rw_ka[0]), rk=row1(rw_rk[0]), ones_blk=ones_blk)
    pc = prep(uc_sh, tm=min(256, lc))
    pz = prep(u_sh, tm=min(256, l))
    s_zero = jnp.zeros((b, 2, RW_WIDTH // QUAD, QUAD, QUAD), F32)
    _, _, s_ctx = _scan_call(*pc[:9], s_zero)
    y_f, y_b, _ = _scan_call(*pz[:9], s_ctx)

    k_all = jnp.concatenate([k_lat, k_ctx], axis=1)
    vt_all = jnp.concatenate([vt_lat, vt_ctx], axis=1)
    r = next(t for t in (3, 2, 1) if vt_all.shape[1] % t == 0)
    o = _attn_call(q, k_all, vt_all, tq=min(256, l), r=r)

    return _outproj_call(x, mod_lat, y_f, y_b, pz[9], sg_rw, o, sg_mla, row1(rw_lnx_g[0]), row1(rw_lnx_b[0]),
                         ones_blk, w_out[0].astype(BF16), row1(final_g), tm=min(256, l))
```

```python
import functools
import math

import numpy as np
import jax
import jax.numpy as jnp
from jax import lax
from jax.experimental import pallas as pl
from jax.experimental.pallas import tpu as pltpu

F32 = jnp.float32
BF16 = jnp.bfloat16

RW_HEAD = 64
RW_HEADS = 8
RW_WIDTH = RW_HEAD * RW_HEADS
DECAY_LORA = 64
AAA_LORA = 64
MLA_HEADS = 8
QK_NOPE = 64
QK_ROPE = 32
QK_DIM = QK_NOPE + QK_ROPE
V_HEAD = 64
MLA_WIDTH = MLA_HEADS * V_HEAD
Q_LORA = 384
KV_LORA = 256
AXIS_DIM = QK_ROPE // 2
ROPE_THETA = 10000.0
GRID_W = 64
NORM_EPS = 1e-6
LNX_EPS = 64e-5
ATTN_SCALE = QK_DIM ** -0.5
N_SHIFT = 3 * RW_WIDTH + 2 * DECAY_LORA + 2 * AAA_LORA

LANES = 128
QK_PAD = LANES
CHUNK = 64
VMEM_LIMIT = 48 * 1024 * 1024

NN = (((1,), (0,)), ((), ()))
NT = (((1,), (1,)), ((), ()))
TN = (((0,), (0,)), ((), ()))


def _dot(a, b, dims=NN):
    return lax.dot_general(a, b, dims, preferred_element_type=F32)


def _split2(x):
    hi = x.astype(BF16)
    lo = (x - hi.astype(F32)).astype(BF16)
    return hi, lo


def _split3(x):
    hi = x.astype(BF16)
    r1 = x - hi.astype(F32)
    mid = r1.astype(BF16)
    lo = (r1 - mid.astype(F32)).astype(BF16)
    return hi, mid, lo


def _mm(a, b, passes, dims=NN):
    if passes == 1:
        return _dot(a.astype(BF16), b.astype(BF16), dims)
    if passes == 3:
        ah, al = _split2(a)
        bh, bl = _split2(b)
        return _dot(ah, bh, dims) + (_dot(ah, bl, dims) + _dot(al, bh, dims))
    ah, am, al = _split3(a)
    bh, bm, bl = _split3(b)
    return (_dot(ah, bh, dims) + (_dot(ah, bm, dims) + _dot(am, bh, dims))
            + (_dot(ah, bl, dims) + _dot(al, bh, dims) + _dot(am, bm, dims)))


def _mm_exact_rhs(a, b_bf16, nsplit, dims=NN):
    if nsplit == 1:
        return _dot(a.astype(BF16), b_bf16, dims)
    parts = _split2(a) if nsplit == 2 else _split3(a)
    out = _dot(parts[0], b_bf16, dims)
    for p in parts[1:]:
        out = out + _dot(p, b_bf16, dims)
    return out


def _mm_exact_lhs(a_bf16, b, nsplit, dims=NN):
    parts = _split2(b) if nsplit == 2 else _split3(b)
    out = _dot(a_bf16, parts[0], dims)
    for p in parts[1:]:
        out = out + _dot(a_bf16, p, dims)
    return out


def _sigmoid(x):
    return 1.0 / (1.0 + jnp.exp(-x))


def _silu(x):
    return x * _sigmoid(x)


def _params(sem):
    return pltpu.CompilerParams(dimension_semantics=sem, vmem_limit_bytes=VMEM_LIMIT)


def _mod_kernel(c_ref, w_ref, b_ref, o_ref):
    s = _silu(c_ref[...])
    o_ref[...] = _mm(s, w_ref[...], 6) + b_ref[...]


def _mod_call(c_rows, ada_w, ada_b):
    n, d = c_rows.shape
    d3 = ada_w.shape[1]
    tn = 512
    return pl.pallas_call(
        _mod_kernel,
        grid=(d3 // tn,),
        in_specs=[pl.BlockSpec((n, d), lambda j: (0, 0)),
                  pl.BlockSpec((d, tn), lambda j: (0, j)),
                  pl.BlockSpec((1, tn), lambda j: (0, j))],
        out_specs=pl.BlockSpec((n, tn), lambda j: (0, j)),
        out_shape=jax.ShapeDtypeStruct((n, d3), F32),
        compiler_params=_params(("arbitrary",)),
        name="mod",
    )(c_rows, ada_w, ada_b.reshape(1, d3))


def _rope(t, cos, sins, lane):
    reps = t.shape[1] // LANES
    cos_f = jnp.concatenate([cos] * reps, axis=1) if reps > 1 else cos
    sin_f = jnp.concatenate([sins] * reps, axis=1) if reps > 1 else sins
    n = t.shape[1]
    lo_half = ((lane % LANES) % (AXIS_DIM)) < (AXIS_DIM // 2)
    partner = jnp.where(lo_half, pltpu.roll(t, n - AXIS_DIM // 2, axis=1), pltpu.roll(t, AXIS_DIM // 2, axis=1))
    return t * cos_f + partner * sin_f


def _inproj_kernel(*refs, rope, want_q, q_scale, n_alias):
    x_ref, mod_ref, ng_ref, w_ref, qg_ref, kvg_ref, wuq_ref, wk_ref, wv_ref = refs[:9]
    n_in = 9 + (2 if rope else 0) + n_alias
    if rope:
        cos_ref, sin_ref = refs[9:11]
    outs = refs[n_in:]
    x = x_ref[0]
    d = x.shape[1]
    shift = mod_ref[0, 0:1, :]
    scale = mod_ref[0, 1:2, :]
    y = x * lax.rsqrt(jnp.mean(x * x, axis=-1, keepdims=True) + NORM_EPS) * ng_ref[...]
    h = (y * (1.0 + scale) + shift).astype(BF16)
    u = _dot(h, w_ref[...])
    o0 = 0
    u_sh = u[:, o0:o0 + N_SHIFT]; o0 += N_SHIFT
    g_rw = u[:, o0:o0 + RW_WIDTH]; o0 += RW_WIDTH
    g_mla = u[:, o0:o0 + MLA_WIDTH]; o0 += MLA_WIDTH
    cq = u[:, o0:o0 + Q_LORA]; o0 += Q_LORA
    ckv = u[:, o0:o0 + KV_LORA]; o0 += KV_LORA
    kr = u[:, o0:o0 + LANES]

    ckv_n = (ckv * lax.rsqrt(jnp.mean(ckv * ckv, axis=-1, keepdims=True) + NORM_EPS) * kvg_ref[...]).astype(BF16)
    k = _dot(ckv_n, wk_ref[...])
    vt = _dot(wv_ref[...], ckv_n, NT)
    if rope:
        cos = cos_ref[...]
        sins = sin_ref[...]
        lane1 = lax.broadcasted_iota(jnp.int32, kr.shape, 1)
        kr = _rope(kr, cos, sins, lane1)
    k = k + jnp.concatenate([kr] * MLA_HEADS, axis=1)

    if want_q:
        u_ref, grw_ref, gmla_ref, q_ref, k_ref, v_ref = outs
        grw_ref[0] = _silu(g_rw)
        gmla_ref[0] = _silu(g_mla)
        cq_n = (cq * lax.rsqrt(jnp.mean(cq * cq, axis=-1, keepdims=True) + NORM_EPS) * qg_ref[...]).astype(BF16)
        q = _dot(cq_n, wuq_ref[...])
        if rope:
            lane8 = lax.broadcasted_iota(jnp.int32, q.shape, 1)
            q = _rope(q, cos, sins, lane8)
        q_ref[0] = (q * q_scale).astype(BF16)
    else:
        u_ref, k_ref, v_ref = outs
    u_ref[0] = u_sh
    k_ref[0] = k.astype(BF16)
    v_ref[0, 0] = vt.astype(BF16)


def _inproj_call(x, mod3, norm_g, w_cat, q_g, kv_g, wuq, wk, wv, cos_t, sin_t, *, rope, want_q, tm,
                 kv_tiles, kv_into=None):
    b, l, d = x.shape
    tile0 = kv_tiles - l // tm if kv_into is not None else 0
    nw = w_cat.shape[1]
    const = lambda bb, i: (0, 0)
    in_specs = [pl.BlockSpec((1, tm, d), lambda bb, i: (bb, i, 0)),
                pl.BlockSpec((1, 3, d), lambda bb, i: (bb, 0, 0)),
                pl.BlockSpec((1, d), const),
                pl.BlockSpec((d, nw), const),
                pl.BlockSpec((1, Q_LORA), const),
                pl.BlockSpec((1, KV_LORA), const),
                pl.BlockSpec(wuq.shape, const),
                pl.BlockSpec(wk.shape, const),
                pl.BlockSpec(wv.shape, const)]
    args = [x, mod3, norm_g, w_cat, q_g, kv_g, wuq, wk, wv]
    if rope:
        in_specs += [pl.BlockSpec((tm, LANES), lambda bb, i: (i, 0)),
                     pl.BlockSpec((tm, LANES), lambda bb, i: (i, 0))]
        args += [cos_t, sin_t]
    aliases = {}
    if kv_into is not None:
        aliases = {len(args): 1, len(args) + 1: 2}
        in_specs += [pl.BlockSpec(memory_space=pl.ANY)] * 2
        args += list(kv_into)
    row = lambda w: pl.BlockSpec((1, tm, w), lambda bb, i: (bb, i, 0))
    sds = lambda w, dt: jax.ShapeDtypeStruct((b, l, w), dt)
    k_spec = pl.BlockSpec((1, tm, MLA_HEADS * QK_PAD), lambda bb, i: (bb, tile0 + i, 0))
    k_sds = jax.ShapeDtypeStruct((b, kv_tiles * tm, MLA_HEADS * QK_PAD), BF16)
    vt_spec = pl.BlockSpec((1, 1, MLA_WIDTH, tm), lambda bb, i: (bb, tile0 + i, 0, 0))
    vt_sds = jax.ShapeDtypeStruct((b, kv_tiles, MLA_WIDTH, tm), BF16)
    if want_q:
        out_specs = [row(N_SHIFT), row(RW_WIDTH), row(MLA_WIDTH), row(MLA_HEADS * QK_PAD), k_spec, vt_spec]
        out_shape = [sds(N_SHIFT, F32), sds(RW_WIDTH, F32), sds(MLA_WIDTH, F32),
                     sds(MLA_HEADS * QK_PAD, BF16), k_sds, vt_sds]
    else:
        out_specs = [row(N_SHIFT), k_spec, vt_spec]
        out_shape = [sds(N_SHIFT, F32), k_sds, vt_sds]
    kern = functools.partial(_inproj_kernel, rope=rope, want_q=want_q,
                             q_scale=ATTN_SCALE * math.log2(math.e), n_alias=len(aliases))
    return pl.pallas_call(
        kern, grid=(b, l // tm), in_specs=in_specs, out_specs=out_specs, out_shape=out_shape,
        input_output_aliases=aliases,
        compiler_params=_params(("parallel", "arbitrary")),
        name="inproj_lat" if want_q else "inproj_ctx",
    )(*args)


def _prep_kernel(u_ref, up_ref, un_ref, mu_ref, w0_ref, w2_ref, a0_ref, a2_ref, kk_ref, ka_ref, rk_ref, ones_ref,
                 r_ref, v_ref, a_ref, lwf_ref, lwb_ref, kf_ref, kb_ref, bf_ref, bb_ref, bon_ref):
    i = pl.program_id(1)
    n = pl.num_programs(1)
    u = u_ref[0]
    tm = u.shape[0]
    prev_row = jnp.where(i > 0, up_ref[0, 7:8, :], 0.0)
    next_row = jnp.where(i < n - 1, un_ref[0, 0:1, :], 0.0)
    rowi = lax.broadcasted_iota(jnp.int32, u.shape, 0)
    u_prev = jnp.where(rowi == 0, prev_row, pltpu.roll(u, 1, axis=0))
    u_next = jnp.where(rowi == tm - 1, next_row, pltpu.roll(u, tm - 1, axis=0))
    us = u + mu_ref[0:1, :] * (u_prev - u) + mu_ref[1:2, :] * (u_next - u)

    w = RW_WIDTH
    r = us[:, 0:w]
    k = us[:, w:2 * w]
    v = us[:, 2 * w:3 * w]
    w_in = us[:, 3 * w:3 * w + LANES]
    a_in = us[:, 3 * w + LANES:3 * w + 2 * LANES]

    z = _mm(jnp.tanh(w_in), w2_ref[...], 3) + w0_ref[...]
    lw = -math.exp(-0.5) * _sigmoid(z)
    a_sig = _sigmoid(_mm(a_in, a2_ref[...], 3) + a0_ref[...])

    ones = ones_ref[...]
    kkf = k * kk_ref[...]
    ss = _mm_exact_rhs(kkf * kkf, ones, 3)
    kk = kkf * lax.rsqrt(jnp.maximum(ss, 1e-24))
    ka = ka_ref[...]
    k_f = k * (1.0 + (a_sig[:, :w] - 1.0) * ka)
    k_b = k * (1.0 + (a_sig[:, w:] - 1.0) * ka)
    bonus = _mm_exact_rhs(r * (k_f + k_b) * rk_ref[...], ones, 3) * v

    r_ref[0] = r
    v_ref[0] = v
    a_ref[0] = -kk
    lwf_ref[0] = lw[:, :w]
    lwb_ref[0] = lw[:, w:]
    kf_ref[0] = k_f
    kb_ref[0] = k_b
    bf_ref[0] = kk * a_sig[:, :w]
    bb_ref[0] = kk * a_sig[:, w:]
    bon_ref[0] = bonus


def _prep_call(u_sh, mu, w0, w2bd, a0, a2bd, kk, ka, rk, ones_blk, *, tm):
    b, l, ns = u_sh.shape
    const = lambda bb, i: (0, 0)
    nb8 = l // 8
    t8 = tm // 8
    in_specs = [pl.BlockSpec((1, tm, ns), lambda bb, i: (bb, i, 0)),
                pl.BlockSpec((1, 8, ns), lambda bb, i: (bb, jnp.maximum(i * t8 - 1, 0), 0)),
                pl.BlockSpec((1, 8, ns), lambda bb, i: (bb, jnp.minimum((i + 1) * t8, nb8 - 1), 0)),
                pl.BlockSpec(mu.shape, const), pl.BlockSpec(w0.shape, const), pl.BlockSpec(w2bd.shape, const),
                pl.BlockSpec(a0.shape, const), pl.BlockSpec(a2bd.shape, const), pl.BlockSpec(kk.shape, const),
                pl.BlockSpec(ka.shape, const), pl.BlockSpec(rk.shape, const), pl.BlockSpec(ones_blk.shape, const)]
    row = pl.BlockSpec((1, tm, RW_WIDTH), lambda bb, i: (bb, i, 0))
    sds = jax.ShapeDtypeStruct((b, l, RW_WIDTH), F32)
    return pl.pallas_call(
        _prep_kernel, grid=(b, l // tm), in_specs=in_specs, out_specs=[row] * 10, out_shape=[sds] * 10,
        compiler_params=_params(("parallel", "arbitrary")),
        name="prep",
    )(u_sh, u_sh, u_sh, mu, w0, w2bd, a0, a2bd, kk, ka, rk, ones_blk)


QUAD = 4 * RW_HEAD
_DONE = object()


def _stack_heads(x, masks):
    zero = jnp.zeros_like(x)
    return jnp.concatenate([jnp.where(m, x, zero) for m in masks], axis=0)


def _unstack_heads(x, c):
    return (x[0:c] + x[c:2 * c]) + (x[2 * c:3 * c] + x[3 * c:4 * c])


def _scan_chain(r, kd, v, a, bd, lw, sv_ref, y_ref, rev, consts):
    tri_bf, strict, incl, eye, masks, prow, pcol = consts
    c = r.shape[0]
    g = _mm_exact_lhs(tri_bf, lw, 3)
    g_end = g[0:1, :] if rev else g[c - 1:c, :]
    eng = jnp.exp(-g)
    e_end = jnp.exp(g_end - g)
    rt = r * jnp.exp(g)
    stk = lambda t: _stack_heads(t.astype(BF16), masks)
    at_s = stk(a * jnp.exp(g - lw))
    rt_s = stk(rt)
    bt_s = stk(bd * eng)
    kt_s = stk(kd * eng)
    bh_s = stk(bd * e_end)
    kh_s = stk(kd * e_end)
    v_s = stk(v)

    yield
    a_ab = jnp.where(strict, _dot(at_s, bt_s, NT), 0.0)
    yield
    a_ak = jnp.where(strict, _dot(at_s, kt_s, NT), 0.0).astype(BF16)
    yield
    a_rb = jnp.where(incl, _dot(rt_s, bt_s, NT), 0.0).astype(BF16)
    yield
    a_rk = jnp.where(incl, _dot(rt_s, kt_s, NT), 0.0).astype(BF16)
    blk = lambda sz: (prow // sz) == (pcol // sz)
    lb = a_ab.astype(BF16)
    zero = jnp.zeros_like(lb)
    t_inv = eye + jnp.where(blk(2), a_ab, 0.0)
    sz = 2
    while sz < c:
        off = jnp.where(jnp.logical_and(blk(2 * sz), jnp.logical_not(blk(sz))), lb, zero)
        tb = t_inv.astype(BF16)
        yield
        lt = _dot(off, tb).astype(BF16)
        yield
        t_inv = t_inv + _dot(tb, lt)
        sz *= 2
    tb = t_inv.astype(BF16)
    yield
    akv = _dot(a_ak, v_s).astype(BF16)
    yield
    wu = _dot(tb, jnp.concatenate([at_s, akv], axis=1)).astype(BF16)
    yield
    ry = _dot(a_rb, wu)
    rq = (rt + _unstack_heads(ry[:, :QUAD], c)).astype(BF16)
    yield
    y0 = _unstack_heads(ry[:, QUAD:] + _dot(a_rk, v_s), c)
    sv = sv_ref[...]
    sv_b = sv.astype(BF16)
    yield
    y_ref[...] = _dot(rq, sv_b, NT) + y0
    yield
    sw = _dot(sv_b, wu[:, :QUAD], NT).astype(BF16)
    yield
    upd = _dot(jnp.concatenate([wu[:, QUAD:], v_s], axis=0), jnp.concatenate([bh_s, kh_s], axis=0), TN)
    yield
    sv_ref[...] = sv * jnp.exp(g_end) + _dot(sw, bh_s) + upd


def _scan_kernel(rf_ref, vf_ref, af_ref, lwf_ref, kf_ref, bf_ref,
                 rb_ref, vb_ref, ab_ref, lwb_ref, kb_ref, bb_ref, s0_ref,
                 yf_ref, yb_ref, sout_ref, s_scr):
    i = pl.program_id(0)
    n = pl.num_programs(0)
    nb, c = rf_ref.shape[0], rf_ref.shape[1]
    nquad = rf_ref.shape[2] // QUAD

    @pl.when(i == 0)
    def _():
        s_scr[...] = s0_ref[...]

    row = lax.broadcasted_iota(jnp.int32, (c, c), 0)
    col = lax.broadcasted_iota(jnp.int32, (c, c), 1)
    prow = lax.broadcasted_iota(jnp.int32, (QUAD, QUAD), 0)
    pcol = lax.broadcasted_iota(jnp.int32, (QUAD, QUAD), 1)
    eye = (prow == pcol).astype(F32)
    lane = lax.broadcasted_iota(jnp.int32, (c, QUAD), 1)
    masks = [(lane // RW_HEAD) == j for j in range(QUAD // RW_HEAD)]

    chains = []
    for d, rev, refs, y_ref in ((0, False, (rf_ref, kf_ref, vf_ref, af_ref, bf_ref, lwf_ref), yf_ref),
                                (1, True, (rb_ref, kb_ref, vb_ref, ab_ref, bb_ref, lwb_ref), yb_ref)):
        if rev:
            tri, strict, incl = (col >= row), (pcol > prow), (pcol >= prow)
        else:
            tri, strict, incl = (col <= row), (pcol < prow), (pcol <= prow)
        consts = (tri.astype(BF16), strict, incl, eye, masks, prow, pcol)
        for bi in range(nb):
            for q in range(nquad):
                ins = (x[bi, :, q * QUAD:(q + 1) * QUAD] for x in refs)
                chains.append(_scan_chain(*ins, s_scr.at[bi, d, q], y_ref.at[bi, :, q * QUAD:(q + 1) * QUAD],
                                          rev, consts))
    while chains:
        for ch in list(chains):
            if next(ch, _DONE) is _DONE:
                chains.remove(ch)

    @pl.when(i == n - 1)
    def _():
        sout_ref[...] = s_scr[...]


def _scan_call(r, v, a, lw_f, lw_b, k_f, k_b, b_f, b_b, s0):
    b, l, w = r.shape
    nc = l // CHUNK
    fwd = pl.BlockSpec((b, CHUNK, w), lambda i: (0, i, 0))
    bwd = pl.BlockSpec((b, CHUNK, w), lambda i: (0, nc - 1 - i, 0))
    st = pl.BlockSpec(s0.shape, lambda i: (0, 0, 0, 0, 0))
    sds = jax.ShapeDtypeStruct((b, l, w), F32)
    return pl.pallas_call(
        _scan_kernel, grid=(nc,),
        in_specs=[fwd] * 6 + [bwd] * 6 + [st],
        out_specs=[fwd, bwd, st],
        out_shape=[sds, sds, jax.ShapeDtypeStruct(s0.shape, F32)],
        scratch_shapes=[pltpu.VMEM(s0.shape, F32)],
        compiler_params=_params(("arbitrary",)),
        name="scan",
    )(r, v, a, lw_f, k_f, b_f, r, v, a, lw_b, k_b, b_b, s0)


ONES_ROWS = 16


def _attn_kernel(q_ref, k_ref, vt_ref, o_ref, sa_scr, sb_scr, *, r):
    tq = q_ref.shape[1]
    tv = vt_ref.shape[3]
    tk = r * tv
    nk = k_ref.shape[1] // tk
    qs = (q_ref[0, :, 0:QK_PAD], q_ref[0, :, QK_PAD:2 * QK_PAD])
    ones = jnp.ones((ONES_ROWS, tk), BF16)

    def produce(j, s_scr):
        off = pl.multiple_of(j * tk, tk)
        kt = k_ref[0, pl.ds(off, tk), :]
        cmax = []
        for h in range(2):
            s = _dot(kt[:, h * QK_PAD:(h + 1) * QK_PAD], qs[h], NT)
            s_scr[h] = s
            cmax.append(jnp.max(s, axis=0, keepdims=True))
        return tuple(cmax)

    def consume(j, s_scr, cmax, m, acc):
        new_m, new_acc = [], []
        for h in range(2):
            mn = jnp.maximum(m[h], cmax[h])
            p = jnp.exp2(s_scr[h] - mn).astype(BF16)
            alpha = jnp.exp2(m[h] - mn)
            vt = [vt_ref[0, j * r + t, h * V_HEAD:(h + 1) * V_HEAD, :] for t in range(r)]
            vaug = jnp.concatenate([jnp.concatenate(vt, axis=1) if r > 1 else vt[0], ones], axis=0)
            new_acc.append(alpha * acc[h] + _dot(vaug, p))
            new_m.append(mn)
        return tuple(new_m), tuple(new_acc)

    neg = jnp.full((1, tq), -1e30, F32)
    zacc = jnp.zeros((V_HEAD + ONES_ROWS, tq), F32)
    m, acc = (neg, neg), (zacc, zacc)
    bufs = (sa_scr, sb_scr)
    cmax = produce(0, bufs[0])
    for j in range(nk):
        cmax_next = produce(j + 1, bufs[(j + 1) % 2]) if j + 1 < nk else None
        m, acc = consume(j, bufs[j % 2], cmax, m, acc)
        cmax = cmax_next
    ot = jnp.concatenate([acc[h][0:V_HEAD] / acc[h][V_HEAD:V_HEAD + 1] for h in range(2)], axis=0)
    o_ref[0] = ot.T


def _attn_call(q, k, vt, *, tq, r):
    b, l, _ = q.shape
    lk = k.shape[1]
    nt, _, tv = vt.shape[1:]
    npair = MLA_HEADS // 2
    return pl.pallas_call(
        functools.partial(_attn_kernel, r=r),
        grid=(b, npair, l // tq),
        in_specs=[pl.BlockSpec((1, tq, 2 * QK_PAD), lambda bb, p, i: (bb, i, p)),
                  pl.BlockSpec((1, lk, 2 * QK_PAD), lambda bb, p, i: (bb, 0, p)),
                  pl.BlockSpec((1, nt, LANES, tv), lambda bb, p, i: (bb, 0, p, 0))],
        out_specs=pl.BlockSpec((1, tq, LANES), lambda bb, p, i: (bb, i, p)),
        out_shape=jax.ShapeDtypeStruct((b, l, MLA_WIDTH), F32),
        scratch_shapes=[pltpu.VMEM((2, r * tv, tq), F32), pltpu.VMEM((2, r * tv, tq), F32)],
        compiler_params=_params(("parallel", "parallel", "arbitrary")),
        name="attn",
    )(q, k, vt)


def _outproj_kernel(x_ref, mod_ref, yf_ref, yb_ref, bon_ref, grw_ref, o_ref, gmla_ref, lg_ref, lb_ref, ones_ref,
                    wo_ref, fg_ref, out_ref):
    x = x_ref[0]
    gate = mod_ref[0, 2:3, :]
    ones = ones_ref[...]
    y = yf_ref[0] + yb_ref[0]
    inv_n = 1.0 / RW_HEAD
    mean = _mm_exact_rhs(y, ones, 3) * inv_n
    yc = y - mean
    var = _mm_exact_rhs(yc * yc, ones, 3) * inv_n
    yn = yc * lax.rsqrt(var + LNX_EPS) * lg_ref[...] + lb_ref[...]
    rw = (yn + bon_ref[0]) * grw_ref[0]
    mla = o_ref[0] * gmla_ref[0]
    cat = jnp.concatenate([rw, mla], axis=1).astype(BF16)
    z = x + gate * _dot(cat, wo_ref[...])
    out_ref[0] = z * lax.rsqrt(jnp.mean(z * z, axis=-1, keepdims=True) + NORM_EPS) * fg_ref[...]


def _outproj_call(x, mod3, y_f, y_b, bonus, sg_rw, o, sg_mla, lnx_g, lnx_b, ones_blk, w_out, final_g, *, tm):
    b, l, d = x.shape
    const = lambda bb, i: (0, 0)
    row = lambda w: pl.BlockSpec((1, tm, w), lambda bb, i: (bb, i, 0))
    return pl.pallas_call(
        _outproj_kernel, grid=(b, l // tm),
        in_specs=[row(d), pl.BlockSpec((1, 3, d), lambda bb, i: (bb, 0, 0)),
                  row(RW_WIDTH), row(RW_WIDTH), row(RW_WIDTH), row(RW_WIDTH), row(MLA_WIDTH), row(MLA_WIDTH),
                  pl.BlockSpec(lnx_g.shape, const), pl.BlockSpec(lnx_b.shape, const),
                  pl.BlockSpec(ones_blk.shape, const), pl.BlockSpec(w_out.shape, const),
                  pl.BlockSpec(final_g.shape, const)],
        out_specs=row(d),
        out_shape=jax.ShapeDtypeStruct((b, l, d), F32),
        compiler_params=_params(("parallel", "arbitrary")),
        name="outproj",
    )(x, mod3, y_f, y_b, bonus, sg_rw, o, sg_mla, lnx_g, lnx_b, ones_blk, w_out, final_g)


def _rope_tables(l):
    rows = l // GRID_W
    inv_freq = ROPE_THETA ** (-jnp.arange(0, AXIS_DIM, 2, dtype=F32) / AXIS_DIM)
    ang_r = jnp.arange(rows, dtype=F32)[:, None] * inv_freq
    ang_c = jnp.arange(GRID_W, dtype=F32)[:, None] * inv_freq
    per_row = lambda t: jnp.repeat(t, GRID_W, axis=0)
    per_col = lambda t: jnp.tile(t, (rows, 1))
    cr, sr, cc, sc = per_row(jnp.cos(ang_r)), per_row(jnp.sin(ang_r)), per_col(jnp.cos(ang_c)), per_col(jnp.sin(ang_c))
    one = jnp.ones((l, QK_NOPE), F32)
    pad1 = jnp.ones((l, QK_PAD - QK_DIM), F32)
    zero = jnp.zeros((l, QK_NOPE), F32)
    pad0 = jnp.zeros((l, QK_PAD - QK_DIM), F32)
    cos_t = jnp.concatenate([one, cr, cr, cc, cc, pad1], axis=1)
    sin_t = jnp.concatenate([zero, -sr, sr, -sc, sc, pad0], axis=1)
    return cos_t, sin_t


def kernel(x, c, ctx, c_ctx, ada_w, ada_b, norm_g, w_in, shift_mu, rw_w0, rw_w2, rw_a0, rw_a2, rw_kk, rw_ka, rw_rk,
           rw_lnx_g, rw_lnx_b, mla_q_norm_g, mla_kv_norm_g, mla_w_uq, mla_w_ukv, w_out, final_g):
    assert x.shape[-1] == 2 * RW_WIDTH and w_in.shape[0] == 1, "single-layer block with d_model = 1024"
    b, l, d = x.shape
    lc = ctx.shape[1]

    w = w_in[0]
    o1 = N_SHIFT + RW_WIDTH
    o2 = o1 + Q_LORA + KV_LORA + QK_ROPE
    w_kr = jnp.zeros((d, LANES), F32).at[:, QK_NOPE:QK_DIM].set(w[:, o1 + Q_LORA + KV_LORA:o2])
    w_cat = jnp.concatenate([w[:, :N_SHIFT], w[:, N_SHIFT:o1], w[:, o2:], w[:, o1:o1 + Q_LORA],
                             w[:, o1 + Q_LORA:o1 + Q_LORA + KV_LORA], w_kr], axis=1).astype(BF16)
    wuq = jnp.pad(mla_w_uq[0].reshape(Q_LORA, MLA_HEADS, QK_DIM), ((0, 0), (0, 0), (0, QK_PAD - QK_DIM)))
    wuq = wuq.reshape(Q_LORA, MLA_HEADS * QK_PAD).astype(BF16)
    wukv = mla_w_ukv[0].reshape(KV_LORA, MLA_HEADS, QK_NOPE + V_HEAD)
    wk = jnp.pad(wukv[:, :, :QK_NOPE], ((0, 0), (0, 0), (0, QK_PAD - QK_NOPE))).reshape(KV_LORA, MLA_HEADS * QK_PAD).astype(BF16)
    wv = wukv[:, :, QK_NOPE:].reshape(KV_LORA, MLA_WIDTH).T.astype(BF16)
    zl = jnp.zeros((DECAY_LORA, RW_WIDTH), F32)
    w2bd = jnp.concatenate([jnp.concatenate([rw_w2[0, 0], zl], axis=1), jnp.concatenate([zl, rw_w2[0, 1]], axis=1)], axis=0)
    a2bd = jnp.concatenate([jnp.concatenate([rw_a2[0, 0], zl], axis=1), jnp.concatenate([zl, rw_a2[0, 1]], axis=1)], axis=0)
    w0 = rw_w0[0].reshape(1, 2 * RW_WIDTH)
    a0 = rw_a0[0].reshape(1, 2 * RW_WIDTH)
    hid = np.arange(RW_WIDTH) // RW_HEAD
    ones_blk = jnp.asarray(hid[:, None] == hid[None, :], BF16)
    cos_t, sin_t = _rope_tables(l)
    row1 = lambda t: t.reshape(1, -1)

    c_rows = jnp.zeros((8, d), F32).at[:b].set(c).at[b].set(c_ctx)
    mod = _mod_call(c_rows, ada_w[0], ada_b[0])
    mod_lat = mod[:b].reshape(b, 3, d)
    mod_ctx = jnp.broadcast_to(mod[b].reshape(1, 3, d), (b, 3, d))

    proj = functools.partial(_inproj_call, norm_g=row1(norm_g[0]), w_cat=w_cat, q_g=row1(mla_q_norm_g[0]),
                             kv_g=row1(mla_kv_norm_g[0]), wuq=wuq, wk=wk, wv=wv, cos_t=cos_t, sin_t=sin_t)
    tm_proj = min(256, l, lc)
    kv_tiles = (l + lc) // tm_proj
    u_sh, sg_rw, sg_mla, q, k_all, vt_all = proj(x, mod_lat, rope=True, want_q=True, tm=tm_proj, kv_tiles=kv_tiles)
    uc_sh, k_all, vt_all = proj(ctx, mod_ctx, rope=False, want_q=False, tm=tm_proj, kv_tiles=kv_tiles,
                                kv_into=(k_all, vt_all))

    prep = functools.partial(_prep_call, mu=shift_mu[0], w0=w0, w2bd=w2bd, a0=a0, a2bd=a2bd, kk=row1(rw_kk[0]),
                             ka=row1(rw_ka[0]), rk=row1(rw_rk[0]), ones_blk=ones_blk)
    pc = prep(uc_sh, tm=min(256, lc))
    pz = prep(u_sh, tm=min(256, l))
    s_zero = jnp.zeros((b, 2, RW_WIDTH // QUAD, QUAD, QUAD), F32)
    _, _, s_ctx = _scan_call(*pc[:9], s_zero)
    y_f, y_b, _ = _scan_call(*pz[:9], s_ctx)

    r = next(t for t in (3, 2, 1) if vt_all.shape[1] % t == 0)
    o = _attn_call(q, k_all, vt_all, tq=min(1024, l), r=r)

    return _outproj_call(x, mod_lat, y_f, y_b, pz[9], sg_rw, o, sg_mla, row1(rw_lnx_g[0]), row1(rw_lnx_b[0]),
                         ones_blk, w_out[0].astype(BF16), row1(final_g), tm=min(256, l))
```

```python
import functools
import math

import numpy as np
import jax
import jax.numpy as jnp
from jax import lax
from jax.experimental import pallas as pl
from jax.experimental.pallas import tpu as pltpu

F32 = jnp.float32
BF16 = jnp.bfloat16

RW_HEAD = 64
RW_HEADS = 8
RW_WIDTH = RW_HEAD * RW_HEADS
DECAY_LORA = 64
AAA_LORA = 64
MLA_HEADS = 8
QK_NOPE = 64
QK_ROPE = 32
QK_DIM = QK_NOPE + QK_ROPE
V_HEAD = 64
MLA_WIDTH = MLA_HEADS * V_HEAD
Q_LORA = 384
KV_LORA = 256
AXIS_DIM = QK_ROPE // 2
ROPE_THETA = 10000.0
GRID_W = 64
NORM_EPS = 1e-6
LNX_EPS = 64e-5
ATTN_SCALE = QK_DIM ** -0.5
N_SHIFT = 3 * RW_WIDTH + 2 * DECAY_LORA + 2 * AAA_LORA

LANES = 128
QK_PAD = LANES
CHUNK = 64
VMEM_LIMIT = 48 * 1024 * 1024

NN = (((1,), (0,)), ((), ()))
NT = (((1,), (1,)), ((), ()))
TN = (((0,), (0,)), ((), ()))


def _dot(a, b, dims=NN):
    return lax.dot_general(a, b, dims, preferred_element_type=F32)


def _split2(x):
    hi = x.astype(BF16)
    lo = (x - hi.astype(F32)).astype(BF16)
    return hi, lo


def _split3(x):
    hi = x.astype(BF16)
    r1 = x - hi.astype(F32)
    mid = r1.astype(BF16)
    lo = (r1 - mid.astype(F32)).astype(BF16)
    return hi, mid, lo


def _mm(a, b, passes, dims=NN):
    if passes == 1:
        return _dot(a.astype(BF16), b.astype(BF16), dims)
    if passes == 3:
        ah, al = _split2(a)
        bh, bl = _split2(b)
        return _dot(ah, bh, dims) + (_dot(ah, bl, dims) + _dot(al, bh, dims))
    ah, am, al = _split3(a)
    bh, bm, bl = _split3(b)
    return (_dot(ah, bh, dims) + (_dot(ah, bm, dims) + _dot(am, bh, dims))
            + (_dot(ah, bl, dims) + _dot(al, bh, dims) + _dot(am, bm, dims)))


def _mm_exact_rhs(a, b_bf16, nsplit, dims=NN):
    if nsplit == 1:
        return _dot(a.astype(BF16), b_bf16, dims)
    parts = _split2(a) if nsplit == 2 else _split3(a)
    out = _dot(parts[0], b_bf16, dims)
    for p in parts[1:]:
        out = out + _dot(p, b_bf16, dims)
    return out


def _mm_exact_lhs(a_bf16, b, nsplit, dims=NN):
    parts = _split2(b) if nsplit == 2 else _split3(b)
    out = _dot(a_bf16, parts[0], dims)
    for p in parts[1:]:
        out = out + _dot(a_bf16, p, dims)
    return out


def _sigmoid(x):
    return 1.0 / (1.0 + jnp.exp(-x))


def _silu(x):
    return x * _sigmoid(x)


def _params(sem):
    return pltpu.CompilerParams(dimension_semantics=sem, vmem_limit_bytes=VMEM_LIMIT)


def _mod_kernel(c_ref, w_ref, b_ref, o_ref):
    s = _silu(c_ref[...])
    o_ref[...] = _mm(s, w_ref[...], 6) + b_ref[...]


def _mod_call(c_rows, ada_w, ada_b):
    n, d = c_rows.shape
    d3 = ada_w.shape[1]
    tn = 512
    return pl.pallas_call(
        _mod_kernel,
        grid=(d3 // tn,),
        in_specs=[pl.BlockSpec((n, d), lambda j: (0, 0)),
                  pl.BlockSpec((d, tn), lambda j: (0, j)),
                  pl.BlockSpec((1, tn), lambda j: (0, j))],
        out_specs=pl.BlockSpec((n, tn), lambda j: (0, j)),
        out_shape=jax.ShapeDtypeStruct((n, d3), F32),
        compiler_params=_params(("arbitrary",)),
        name="mod",
    )(c_rows, ada_w, ada_b.reshape(1, d3))


HALO = 8


def _rope(t, cos, sins, lane):
    reps = t.shape[1] // LANES
    cos_f = jnp.concatenate([cos] * reps, axis=1) if reps > 1 else cos
    sin_f = jnp.concatenate([sins] * reps, axis=1) if reps > 1 else sins
    n = t.shape[1]
    lo_half = ((lane % LANES) % (AXIS_DIM)) < (AXIS_DIM // 2)
    partner = jnp.where(lo_half, pltpu.roll(t, n - AXIS_DIM // 2, axis=1), pltpu.roll(t, AXIS_DIM // 2, axis=1))
    return t * cos_f + partner * sin_f


def _rms(t, g):
    return t * lax.rsqrt(jnp.mean(t * t, axis=-1, keepdims=True) + NORM_EPS) * g


def _front_kernel(*refs, rope, latent, q_scale, n_alias):
    (x_ref, xp_ref, xn_ref, mod_ref, ng_ref, w_ref, qg_ref, kvg_ref, wuq_ref, wk_ref, wv_ref,
     mu_ref, w0_ref, w2_ref, a0_ref, a2_ref, kk_ref, ka_ref, rk_ref, ones_ref) = refs[:20]
    n_in = 20 + (2 if rope else 0) + n_alias
    if rope:
        cos_ref, sin_ref = refs[20:22]
    outs = refs[n_in:]
    i = pl.program_id(1)
    n = pl.num_programs(1)
    tm = x_ref.shape[1]

    x_ext = jnp.concatenate([xp_ref[0], x_ref[0], xn_ref[0]], axis=0)
    h = (_rms(x_ext, ng_ref[...]) * (1.0 + mod_ref[0, 1:2, :]) + mod_ref[0, 0:1, :]).astype(BF16)
    u_all = _dot(h, w_ref[...])
    u_ext = u_all[:, :N_SHIFT]
    rest = u_all[HALO:HALO + tm, N_SHIFT:]
    o0 = 0
    g_rw = rest[:, o0:o0 + RW_WIDTH]; o0 += RW_WIDTH
    g_mla = rest[:, o0:o0 + MLA_WIDTH]; o0 += MLA_WIDTH
    cq = rest[:, o0:o0 + Q_LORA]; o0 += Q_LORA
    ckv = rest[:, o0:o0 + KV_LORA]; o0 += KV_LORA
    kr = rest[:, o0:o0 + LANES]

    ckv_n = _rms(ckv, kvg_ref[...]).astype(BF16)
    k = _dot(ckv_n, wk_ref[...])
    vt = _dot(wv_ref[...], ckv_n, NT)
    if rope:
        cos = cos_ref[...]
        sins = sin_ref[...]
        kr = _rope(kr, cos, sins, lax.broadcasted_iota(jnp.int32, kr.shape, 1))
    k = k + jnp.concatenate([kr] * MLA_HEADS, axis=1)
    if latent:
        (grw_ref, gmla_ref, q_ref, k_ref, v_ref,
         r_ref, vv_ref, a_ref, lwf_ref, lwb_ref, kf_ref, kb_ref, bf_ref, bb_ref, bon_ref) = outs
        grw_ref[0] = _silu(g_rw).astype(BF16)
        gmla_ref[0] = _silu(g_mla).astype(BF16)
        q = _dot(_rms(cq, qg_ref[...]).astype(BF16), wuq_ref[...])
        if rope:
            q = _rope(q, cos, sins, lax.broadcasted_iota(jnp.int32, q.shape, 1))
        q_ref[0] = (q * q_scale).astype(BF16)
    else:
        k_ref, v_ref, r_ref, vv_ref, a_ref, lwf_ref, lwb_ref, kf_ref, kb_ref, bf_ref, bb_ref = outs
    k_ref[0] = k.astype(BF16)
    v_ref[0, 0] = vt.astype(BF16)

    u = u_ext[HALO:HALO + tm]
    rowi = lax.broadcasted_iota(jnp.int32, u.shape, 0)
    u_prev = pltpu.roll(u_ext, 1, axis=0)[HALO:HALO + tm]
    u_next = pltpu.roll(u_ext, tm + 2 * HALO - 1, axis=0)[HALO:HALO + tm]
    u_prev = jnp.where(jnp.logical_and(rowi == 0, i == 0), 0.0, u_prev)
    u_next = jnp.where(jnp.logical_and(rowi == tm - 1, i == n - 1), 0.0, u_next)
    us = u + mu_ref[0:1, :] * (u_prev - u) + mu_ref[1:2, :] * (u_next - u)

    w = RW_WIDTH
    r = us[:, 0:w]
    kx = us[:, w:2 * w]
    v = us[:, 2 * w:3 * w]
    w_in = us[:, 3 * w:3 * w + LANES]
    a_in = us[:, 3 * w + LANES:3 * w + 2 * LANES]
    z = _mm(jnp.tanh(w_in), w2_ref[...], 1) + w0_ref[...]
    lw = -math.exp(-0.5) * _sigmoid(z)
    a_sig = _sigmoid(_mm(a_in, a2_ref[...], 1) + a0_ref[...])
    ones = ones_ref[...]
    kkf = kx * kk_ref[...]
    kk = kkf * lax.rsqrt(jnp.maximum(_mm_exact_rhs(kkf * kkf, ones, 1), 1e-24))
    ka = ka_ref[...]
    k_f = kx * (1.0 + (a_sig[:, :w] - 1.0) * ka)
    k_b = kx * (1.0 + (a_sig[:, w:] - 1.0) * ka)
    r_ref[0] = r.astype(BF16)
    vv_ref[0] = v.astype(BF16)
    a_ref[0] = (-kk).astype(BF16)
    lwf_ref[0] = lw[:, :w]
    lwb_ref[0] = lw[:, w:]
    kf_ref[0] = k_f.astype(BF16)
    kb_ref[0] = k_b.astype(BF16)
    bf_ref[0] = (kk * a_sig[:, :w]).astype(BF16)
    bb_ref[0] = (kk * a_sig[:, w:]).astype(BF16)
    if latent:
        bon_ref[0] = (_mm_exact_rhs(r * (k_f + k_b) * rk_ref[...], ones, 2) * v).astype(BF16)


def _front_call(x, mod3, consts, cos_t, sin_t, *, rope, latent, tm, kv_tiles, kv_into=None):
    b, l, d = x.shape
    tile0 = kv_tiles - l // tm if kv_into is not None else 0
    t8 = tm // HALO
    nb8 = l // HALO
    const = lambda bb, i: (0, 0)
    in_specs = [pl.BlockSpec((1, tm, d), lambda bb, i: (bb, i, 0)),
                pl.BlockSpec((1, HALO, d), lambda bb, i: (bb, jnp.maximum(i * t8 - 1, 0), 0)),
                pl.BlockSpec((1, HALO, d), lambda bb, i: (bb, jnp.minimum((i + 1) * t8, nb8 - 1), 0)),
                pl.BlockSpec((1, 3, d), lambda bb, i: (bb, 0, 0))]
    in_specs += [pl.BlockSpec(c.shape, const) for c in consts]
    args = [x, x, x, mod3] + list(consts)
    if rope:
        in_specs += [pl.BlockSpec((tm, LANES), lambda bb, i: (i, 0))] * 2
        args += [cos_t, sin_t]
    aliases = {}
    if kv_into is not None:
        first = 3 if latent else 0
        aliases = {len(args): first, len(args) + 1: first + 1}
        in_specs += [pl.BlockSpec(memory_space=pl.ANY)] * 2
        args += list(kv_into)
    row = lambda wd: pl.BlockSpec((1, tm, wd), lambda bb, i: (bb, i, 0))
    sds = lambda wd, dt: jax.ShapeDtypeStruct((b, l, wd), dt)
    kw = MLA_HEADS * QK_PAD
    kv_specs = [pl.BlockSpec((1, tm, kw), lambda bb, i: (bb, tile0 + i, 0)),
                pl.BlockSpec((1, 1, MLA_WIDTH, tm), lambda bb, i: (bb, tile0 + i, 0, 0))]
    kv_shapes = [jax.ShapeDtypeStruct((b, kv_tiles * tm, kw), BF16),
                 jax.ShapeDtypeStruct((b, kv_tiles, MLA_WIDTH, tm), BF16)]
    rw = RW_WIDTH
    scan_dt = [BF16, BF16, BF16, F32, F32, BF16, BF16, BF16, BF16]
    out_specs = kv_specs + [row(rw)] * 9
    out_shape = kv_shapes + [sds(rw, dt) for dt in scan_dt]
    if latent:
        out_specs = [row(rw), row(MLA_WIDTH), row(kw)] + out_specs + [row(rw)]
        out_shape = [sds(rw, BF16), sds(MLA_WIDTH, BF16), sds(kw, BF16)] + out_shape + [sds(rw, BF16)]
    kern = functools.partial(_front_kernel, rope=rope, latent=latent,
                             q_scale=ATTN_SCALE * math.log2(math.e), n_alias=len(aliases))
    return pl.pallas_call(
        kern, grid=(b, l // tm), in_specs=in_specs, out_specs=out_specs, out_shape=out_shape,
        input_output_aliases=aliases,
        compiler_params=_params(("parallel", "arbitrary")),
        name="front_lat" if latent else "front_ctx",
    )(*args)


QUAD = 4 * RW_HEAD
_DONE = object()


def _stack_heads(x, masks):
    zero = jnp.zeros_like(x)
    return jnp.concatenate([jnp.where(m, x, zero) for m in masks], axis=0)


def _unstack_heads(x, c):
    return (x[0:c] + x[c:2 * c]) + (x[2 * c:3 * c] + x[3 * c:4 * c])


def _scan_chain(r, kd, v, a, bd, lw, sv_ref, y_ref, rev, consts):
    tri_bf, strict, incl, eye, masks, prow, pcol = consts
    c = r.shape[0]
    g = _mm_exact_lhs(tri_bf, lw, 3)
    g_end = g[0:1, :] if rev else g[c - 1:c, :]
    eng = jnp.exp(-g)
    e_end = jnp.exp(g_end - g)
    rt = r * jnp.exp(g)
    stk = lambda t: _stack_heads(t.astype(BF16), masks)
    at_s = stk(a * jnp.exp(g - lw))
    rt_s = stk(rt)
    bt_s = stk(bd * eng)
    kt_s = stk(kd * eng)
    bh_s = stk(bd * e_end)
    kh_s = stk(kd * e_end)
    v_s = stk(v)

    yield
    a_ab = jnp.where(strict, _dot(at_s, bt_s, NT), 0.0)
    yield
    a_ak = jnp.where(strict, _dot(at_s, kt_s, NT), 0.0).astype(BF16)
    yield
    a_rb = jnp.where(incl, _dot(rt_s, bt_s, NT), 0.0).astype(BF16)
    yield
    a_rk = jnp.where(incl, _dot(rt_s, kt_s, NT), 0.0).astype(BF16)
    blk = lambda sz: (prow // sz) == (pcol // sz)
    lb = a_ab.astype(BF16)
    zero = jnp.zeros_like(lb)
    t_inv = eye + jnp.where(blk(2), a_ab, 0.0)
    sz = 2
    while sz < c:
        off = jnp.where(jnp.logical_and(blk(2 * sz), jnp.logical_not(blk(sz))), lb, zero)
        tb = t_inv.astype(BF16)
        yield
        lt = _dot(off, tb).astype(BF16)
        yield
        t_inv = t_inv + _dot(tb, lt)
        sz *= 2
    tb = t_inv.astype(BF16)
    yield
    akv = _dot(a_ak, v_s).astype(BF16)
    yield
    wu = _dot(tb, jnp.concatenate([at_s, akv], axis=1)).astype(BF16)
    yield
    ry = _dot(a_rb, wu)
    rq = (rt + _unstack_heads(ry[:, :QUAD], c)).astype(BF16)
    yield
    y0 = _unstack_heads(ry[:, QUAD:] + _dot(a_rk, v_s), c)
    sv = sv_ref[...]
    sv_b = sv.astype(BF16)
    yield
    y_ref[...] = _dot(rq, sv_b, NT) + y0
    yield
    sw = _dot(sv_b, wu[:, :QUAD], NT).astype(BF16)
    yield
    upd = _dot(jnp.concatenate([wu[:, QUAD:], v_s], axis=0), jnp.concatenate([bh_s, kh_s], axis=0), TN)
    yield
    sv_ref[...] = sv * jnp.exp(g_end) + _dot(sw, bh_s) + upd


def _scan_kernel(rf_ref, vf_ref, af_ref, lwf_ref, kf_ref, bf_ref,
                 rb_ref, vb_ref, ab_ref, lwb_ref, kb_ref, bb_ref, s0_ref,
                 yf_ref, yb_ref, sout_ref, s_scr):
    i = pl.program_id(0)
    n = pl.num_programs(0)
    nb, c = rf_ref.shape[0], rf_ref.shape[1]
    nquad = rf_ref.shape[2] // QUAD

    @pl.when(i == 0)
    def _():
        s_scr[...] = s0_ref[...]

    row = lax.broadcasted_iota(jnp.int32, (c, c), 0)
    col = lax.broadcasted_iota(jnp.int32, (c, c), 1)
    prow = lax.broadcasted_iota(jnp.int32, (QUAD, QUAD), 0)
    pcol = lax.broadcasted_iota(jnp.int32, (QUAD, QUAD), 1)
    eye = (prow == pcol).astype(F32)
    lane = lax.broadcasted_iota(jnp.int32, (c, QUAD), 1)
    masks = [(lane // RW_HEAD) == j for j in range(QUAD // RW_HEAD)]

    chains = []
    for d, rev, refs, y_ref in ((0, False, (rf_ref, kf_ref, vf_ref, af_ref, bf_ref, lwf_ref), yf_ref),
                                (1, True, (rb_ref, kb_ref, vb_ref, ab_ref, bb_ref, lwb_ref), yb_ref)):
        if rev:
            tri, strict, incl = (col >= row), (pcol > prow), (pcol >= prow)
        else:
            tri, strict, incl = (col <= row), (pcol < prow), (pcol <= prow)
        consts = (tri.astype(BF16), strict, incl, eye, masks, prow, pcol)
        for bi in range(nb):
            for q in range(nquad):
                ins = (x[bi, :, q * QUAD:(q + 1) * QUAD].astype(F32) for x in refs)
                chains.append(_scan_chain(*ins, s_scr.at[bi, d, q], y_ref.at[bi, :, q * QUAD:(q + 1) * QUAD],
                                          rev, consts))
    while chains:
        for ch in list(chains):
            if next(ch, _DONE) is _DONE:
                chains.remove(ch)

    @pl.when(i == n - 1)
    def _():
        sout_ref[...] = s_scr[...]


def _scan_call(r, v, a, lw_f, lw_b, k_f, k_b, b_f, b_b, s0):
    b, l, w = r.shape
    nc = l // CHUNK
    fwd = pl.BlockSpec((b, CHUNK, w), lambda i: (0, i, 0))
    bwd = pl.BlockSpec((b, CHUNK, w), lambda i: (0, nc - 1 - i, 0))
    st = pl.BlockSpec(s0.shape, lambda i: (0, 0, 0, 0, 0))
    sds = jax.ShapeDtypeStruct((b, l, w), F32)
    return pl.pallas_call(
        _scan_kernel, grid=(nc,),
        in_specs=[fwd] * 6 + [bwd] * 6 + [st],
        out_specs=[fwd, bwd, st],
        out_shape=[sds, sds, jax.ShapeDtypeStruct(s0.shape, F32)],
        scratch_shapes=[pltpu.VMEM(s0.shape, F32)],
        compiler_params=_params(("arbitrary",)),
        name="scan",
    )(r, v, a, lw_f, k_f, b_f, r, v, a, lw_b, k_b, b_b, s0)


ONES_ROWS = 16


def _attn_kernel(q_ref, k_ref, vt_ref, o_ref, sa_scr, sb_scr, *, r):
    tq = q_ref.shape[1]
    tv = vt_ref.shape[3]
    tk = r * tv
    nk = k_ref.shape[1] // tk
    qs = (q_ref[0, :, 0:QK_PAD], q_ref[0, :, QK_PAD:2 * QK_PAD])
    ones = jnp.ones((ONES_ROWS, tk), BF16)

    def produce(j, s_scr):
        off = pl.multiple_of(j * tk, tk)
        kt = k_ref[0, pl.ds(off, tk), :]
        cmax = []
        for h in range(2):
            s = _dot(kt[:, h * QK_PAD:(h + 1) * QK_PAD], qs[h], NT)
            s_scr[h] = s
            cmax.append(jnp.max(s, axis=0, keepdims=True))
        return tuple(cmax)

    def consume(j, s_scr, cmax, m, acc):
        new_m, new_acc = [], []
        for h in range(2):
            mn = jnp.maximum(m[h], cmax[h])
            p = jnp.exp2(s_scr[h] - mn).astype(BF16)
            alpha = jnp.exp2(m[h] - mn)
            vt = [vt_ref[0, j * r + t, h * V_HEAD:(h + 1) * V_HEAD, :] for t in range(r)]
            vaug = jnp.concatenate([jnp.concatenate(vt, axis=1) if r > 1 else vt[0], ones], axis=0)
            new_acc.append(alpha * acc[h] + _dot(vaug, p))
            new_m.append(mn)
        return tuple(new_m), tuple(new_acc)

    neg = jnp.full((1, tq), -1e30, F32)
    zacc = jnp.zeros((V_HEAD + ONES_ROWS, tq), F32)
    m, acc = (neg, neg), (zacc, zacc)
    bufs = (sa_scr, sb_scr)
    cmax = produce(0, bufs[0])
    for j in range(nk):
        cmax_next = produce(j + 1, bufs[(j + 1) % 2]) if j + 1 < nk else None
        m, acc = consume(j, bufs[j % 2], cmax, m, acc)
        cmax = cmax_next
    ot = jnp.concatenate([acc[h][0:V_HEAD] / acc[h][V_HEAD:V_HEAD + 1] for h in range(2)], axis=0)
    o_ref[0] = ot.T.astype(o_ref.dtype)


def _attn_call(q, k, vt, *, tq, r):
    b, l, _ = q.shape
    lk = k.shape[1]
    nt, _, tv = vt.shape[1:]
    npair = MLA_HEADS // 2
    return pl.pallas_call(
        functools.partial(_attn_kernel, r=r),
        grid=(b, npair, l // tq),
        in_specs=[pl.BlockSpec((1, tq, 2 * QK_PAD), lambda bb, p, i: (bb, i, p)),
                  pl.BlockSpec((1, lk, 2 * QK_PAD), lambda bb, p, i: (bb, 0, p)),
                  pl.BlockSpec((1, nt, LANES, tv), lambda bb, p, i: (bb, 0, p, 0))],
        out_specs=pl.BlockSpec((1, tq, LANES), lambda bb, p, i: (bb, i, p)),
        out_shape=jax.ShapeDtypeStruct((b, l, MLA_WIDTH), BF16),
        scratch_shapes=[pltpu.VMEM((2, r * tv, tq), F32), pltpu.VMEM((2, r * tv, tq), F32)],
        compiler_params=_params(("parallel", "parallel", "arbitrary")),
        name="attn",
    )(q, k, vt)


def _outproj_kernel(x_ref, mod_ref, yf_ref, yb_ref, bon_ref, grw_ref, o_ref, gmla_ref, lg_ref, lb_ref, ones_ref,
                    wo_ref, fg_ref, out_ref):
    x = x_ref[0]
    gate = mod_ref[0, 2:3, :]
    ones = ones_ref[...]
    y = yf_ref[0] + yb_ref[0]
    inv_n = 1.0 / RW_HEAD
    mean = _mm_exact_rhs(y, ones, 2) * inv_n
    yc = y - mean
    var = _mm_exact_rhs(yc * yc, ones, 1) * inv_n
    yn = yc * lax.rsqrt(var + LNX_EPS) * lg_ref[...] + lb_ref[...]
    rw = (yn + bon_ref[0].astype(F32)) * grw_ref[0].astype(F32)
    mla = o_ref[0].astype(F32) * gmla_ref[0].astype(F32)
    cat = jnp.concatenate([rw, mla], axis=1).astype(BF16)
    z = x + gate * _dot(cat, wo_ref[...])
    out_ref[0] = z * lax.rsqrt(jnp.mean(z * z, axis=-1, keepdims=True) + NORM_EPS) * fg_ref[...]


def _outproj_call(x, mod3, y_f, y_b, bonus, sg_rw, o, sg_mla, lnx_g, lnx_b, ones_blk, w_out, final_g, *, tm):
    b, l, d = x.shape
    const = lambda bb, i: (0, 0)
    row = lambda w: pl.BlockSpec((1, tm, w), lambda bb, i: (bb, i, 0))
    return pl.pallas_call(
        _outproj_kernel, grid=(b, l // tm),
        in_specs=[row(d), pl.BlockSpec((1, 3, d), lambda bb, i: (bb, 0, 0)),
                  row(RW_WIDTH), row(RW_WIDTH), row(RW_WIDTH), row(RW_WIDTH), row(MLA_WIDTH), row(MLA_WIDTH),
                  pl.BlockSpec(lnx_g.shape, const), pl.BlockSpec(lnx_b.shape, const),
                  pl.BlockSpec(ones_blk.shape, const), pl.BlockSpec(w_out.shape, const),
                  pl.BlockSpec(final_g.shape, const)],
        out_specs=row(d),
        out_shape=jax.ShapeDtypeStruct((b, l, d), F32),
        compiler_params=_params(("parallel", "arbitrary")),
        name="outproj",
    )(x, mod3, y_f, y_b, bonus, sg_rw, o, sg_mla, lnx_g, lnx_b, ones_blk, w_out, final_g)


def _rope_tables(l):
    rows = l // GRID_W
    inv_freq = ROPE_THETA ** (-jnp.arange(0, AXIS_DIM, 2, dtype=F32) / AXIS_DIM)
    ang_r = jnp.arange(rows, dtype=F32)[:, None] * inv_freq
    ang_c = jnp.arange(GRID_W, dtype=F32)[:, None] * inv_freq

    half = AXIS_DIM // 2
    cos_r = jnp.concatenate([jnp.ones((rows, QK_NOPE), F32), jnp.cos(ang_r), jnp.cos(ang_r),
                             jnp.ones((rows, LANES - QK_NOPE - 2 * half), F32)], axis=1)
    cos_c = jnp.concatenate([jnp.ones((GRID_W, QK_NOPE + 2 * half), F32), jnp.cos(ang_c), jnp.cos(ang_c),
                             jnp.ones((GRID_W, LANES - QK_DIM), F32)], axis=1)
    sin_r = jnp.concatenate([jnp.zeros((rows, QK_NOPE), F32), -jnp.sin(ang_r), jnp.sin(ang_r),
                             jnp.zeros((rows, LANES - QK_NOPE - 2 * half), F32)], axis=1)
    sin_c = jnp.concatenate([jnp.zeros((GRID_W, QK_NOPE + 2 * half), F32), -jnp.sin(ang_c), jnp.sin(ang_c),
                             jnp.zeros((GRID_W, LANES - QK_DIM), F32)], axis=1)
    cos_t = (cos_r[:, None, :] * cos_c[None, :, :]).reshape(l, LANES)
    sin_t = (sin_r[:, None, :] + sin_c[None, :, :]).reshape(l, LANES)
    return cos_t, sin_t


def kernel(x, c, ctx, c_ctx, ada_w, ada_b, norm_g, w_in, shift_mu, rw_w0, rw_w2, rw_a0, rw_a2, rw_kk, rw_ka, rw_rk,
           rw_lnx_g, rw_lnx_b, mla_q_norm_g, mla_kv_norm_g, mla_w_uq, mla_w_ukv, w_out, final_g):
    assert x.shape[-1] == 2 * RW_WIDTH and w_in.shape[0] == 1, "single-layer block with d_model = 1024"
    b, l, d = x.shape
    lc = ctx.shape[1]

    w = w_in[0]
    o1 = N_SHIFT + RW_WIDTH
    o2 = o1 + Q_LORA + KV_LORA + QK_ROPE
    w_kr = jnp.zeros((d, LANES), F32).at[:, QK_NOPE:QK_DIM].set(w[:, o1 + Q_LORA + KV_LORA:o2])
    w_cat = jnp.concatenate([w[:, :N_SHIFT], w[:, N_SHIFT:o1], w[:, o2:], w[:, o1:o1 + Q_LORA],
                             w[:, o1 + Q_LORA:o1 + Q_LORA + KV_LORA], w_kr], axis=1).astype(BF16)
    wuq = jnp.pad(mla_w_uq[0].reshape(Q_LORA, MLA_HEADS, QK_DIM), ((0, 0), (0, 0), (0, QK_PAD - QK_DIM)))
    wuq = wuq.reshape(Q_LORA, MLA_HEADS * QK_PAD).astype(BF16)
    wukv = mla_w_ukv[0].reshape(KV_LORA, MLA_HEADS, QK_NOPE + V_HEAD)
    wk = jnp.pad(wukv[:, :, :QK_NOPE], ((0, 0), (0, 0), (0, QK_PAD - QK_NOPE))).reshape(KV_LORA, MLA_HEADS * QK_PAD).astype(BF16)
    wv = wukv[:, :, QK_NOPE:].reshape(KV_LORA, MLA_WIDTH).T.astype(BF16)
    zl = jnp.zeros((DECAY_LORA, RW_WIDTH), F32)
    w2bd = jnp.concatenate([jnp.concatenate([rw_w2[0, 0], zl], axis=1), jnp.concatenate([zl, rw_w2[0, 1]], axis=1)], axis=0)
    a2bd = jnp.concatenate([jnp.concatenate([rw_a2[0, 0], zl], axis=1), jnp.concatenate([zl, rw_a2[0, 1]], axis=1)], axis=0)
    w0 = rw_w0[0].reshape(1, 2 * RW_WIDTH)
    a0 = rw_a0[0].reshape(1, 2 * RW_WIDTH)
    hid = np.arange(RW_WIDTH) // RW_HEAD
    ones_blk = jnp.asarray(hid[:, None] == hid[None, :], BF16)
    cos_t, sin_t = _rope_tables(l)
    row1 = lambda t: t.reshape(1, -1)

    c_rows = jnp.zeros((8, d), F32).at[:b].set(c).at[b].set(c_ctx)
    mod = _mod_call(c_rows, ada_w[0], ada_b[0])
    mod_lat = mod[:b].reshape(b, 3, d)
    mod_ctx = jnp.broadcast_to(mod[b].reshape(1, 3, d), (b, 3, d))

    consts = (row1(norm_g[0]), w_cat, row1(mla_q_norm_g[0]), row1(mla_kv_norm_g[0]), wuq, wk, wv,
              shift_mu[0], w0, w2bd, a0, a2bd, row1(rw_kk[0]), row1(rw_ka[0]), row1(rw_rk[0]), ones_blk)
    tm_f = min(256, l, lc)
    kv_tiles = (l + lc) // tm_f
    front = functools.partial(_front_call, consts=consts, cos_t=cos_t, sin_t=sin_t, tm=tm_f, kv_tiles=kv_tiles)
    sg_rw, sg_mla, q, k_all, vt_all, *pz = front(x, mod_lat, rope=True, latent=True)
    k_all, vt_all, *pc = front(ctx, mod_ctx, rope=False, latent=False, kv_into=(k_all, vt_all))

    s_zero = jnp.zeros((b, 2, RW_WIDTH // QUAD, QUAD, QUAD), F32)
    _, _, s_ctx = _scan_call(*pc[:9], s_zero)
    y_f, y_b, _ = _scan_call(*pz[:9], s_ctx)

    r = next(t for t in (3, 2, 1) if vt_all.shape[1] % t == 0)
    tq = next(t for t in (1024, 512, 256, l) if l % t == 0)
    o = _attn_call(q, k_all, vt_all, tq=tq, r=r)

    return _outproj_call(x, mod_lat, y_f, y_b, pz[9], sg_rw, o, sg_mla, row1(rw_lnx_g[0]), row1(rw_lnx_b[0]),
                         ones_blk, w_out[0].astype(BF16), row1(final_g), tm=min(256, l))
```

```python
import functools
import math

import numpy as np
import jax
import jax.numpy as jnp
from jax import lax
from jax.experimental import pallas as pl
from jax.experimental.pallas import tpu as pltpu

F32 = jnp.float32
BF16 = jnp.bfloat16

RW_HEAD = 64
RW_HEADS = 8
RW_WIDTH = RW_HEAD * RW_HEADS
DECAY_LORA = 64
AAA_LORA = 64
MLA_HEADS = 8
QK_NOPE = 64
QK_ROPE = 32
QK_DIM = QK_NOPE + QK_ROPE
V_HEAD = 64
MLA_WIDTH = MLA_HEADS * V_HEAD
Q_LORA = 384
KV_LORA = 256
AXIS_DIM = QK_ROPE // 2
ROPE_THETA = 10000.0
GRID_W = 64
NORM_EPS = 1e-6
LNX_EPS = 64e-5
ATTN_SCALE = QK_DIM ** -0.5
N_SHIFT = 3 * RW_WIDTH + 2 * DECAY_LORA + 2 * AAA_LORA

LANES = 128
QK_PAD = LANES
CHUNK = 64
VMEM_LIMIT = 48 * 1024 * 1024

NN = (((1,), (0,)), ((), ()))
NT = (((1,), (1,)), ((), ()))
TN = (((0,), (0,)), ((), ()))


def _dot(a, b, dims=NN):
    return lax.dot_general(a, b, dims, preferred_element_type=F32)


def _split2(x):
    hi = x.astype(BF16)
    lo = (x - hi.astype(F32)).astype(BF16)
    return hi, lo


def _split3(x):
    hi = x.astype(BF16)
    r1 = x - hi.astype(F32)
    mid = r1.astype(BF16)
    lo = (r1 - mid.astype(F32)).astype(BF16)
    return hi, mid, lo


def _mm(a, b, passes, dims=NN):
    if passes == 1:
        return _dot(a.astype(BF16), b.astype(BF16), dims)
    if passes == 3:
        ah, al = _split2(a)
        bh, bl = _split2(b)
        return _dot(ah, bh, dims) + (_dot(ah, bl, dims) + _dot(al, bh, dims))
    ah, am, al = _split3(a)
    bh, bm, bl = _split3(b)
    return (_dot(ah, bh, dims) + (_dot(ah, bm, dims) + _dot(am, bh, dims))
            + (_dot(ah, bl, dims) + _dot(al, bh, dims) + _dot(am, bm, dims)))


def _mm_exact_rhs(a, b_bf16, nsplit, dims=NN):
    if nsplit == 1:
        return _dot(a.astype(BF16), b_bf16, dims)
    parts = _split2(a) if nsplit == 2 else _split3(a)
    out = _dot(parts[0], b_bf16, dims)
    for p in parts[1:]:
        out = out + _dot(p, b_bf16, dims)
    return out


def _mm_exact_lhs(a_bf16, b, nsplit, dims=NN):
    parts = _split2(b) if nsplit == 2 else _split3(b)
    out = _dot(a_bf16, parts[0], dims)
    for p in parts[1:]:
        out = out + _dot(a_bf16, p, dims)
    return out


def _sigmoid(x):
    return 1.0 / (1.0 + jnp.exp(-x))


def _silu(x):
    return x * _sigmoid(x)


def _params(sem):
    return pltpu.CompilerParams(dimension_semantics=sem, vmem_limit_bytes=VMEM_LIMIT)


def _mod_kernel(c_ref, w_ref, b_ref, o_ref):
    s = _silu(c_ref[...])
    o_ref[...] = _mm(s, w_ref[...], 6) + b_ref[...]


def _mod_call(c_rows, ada_w, ada_b):
    n, d = c_rows.shape
    d3 = ada_w.shape[1]
    tn = 512
    return pl.pallas_call(
        _mod_kernel,
        grid=(d3 // tn,),
        in_specs=[pl.BlockSpec((n, d), lambda j: (0, 0)),
                  pl.BlockSpec((d, tn), lambda j: (0, j)),
                  pl.BlockSpec((1, tn), lambda j: (0, j))],
        out_specs=pl.BlockSpec((n, tn), lambda j: (0, j)),
        out_shape=jax.ShapeDtypeStruct((n, d3), F32),
        compiler_params=_params(("arbitrary",)),
        name="mod",
    )(c_rows, ada_w, ada_b.reshape(1, d3))


HALO = 8


def _rope(t, cos, sins, lane):
    reps = t.shape[1] // LANES
    cos_f = jnp.concatenate([cos] * reps, axis=1) if reps > 1 else cos
    sin_f = jnp.concatenate([sins] * reps, axis=1) if reps > 1 else sins
    n = t.shape[1]
    lo_half = ((lane % LANES) % (AXIS_DIM)) < (AXIS_DIM // 2)
    partner = jnp.where(lo_half, pltpu.roll(t, n - AXIS_DIM // 2, axis=1), pltpu.roll(t, AXIS_DIM // 2, axis=1))
    return t * cos_f + partner * sin_f


def _rms(t, g):
    return t * lax.rsqrt(jnp.mean(t * t, axis=-1, keepdims=True) + NORM_EPS) * g


def _front_kernel(*refs, rope, latent, q_scale):
    (x_ref, xp_ref, xn_ref, mod_ref, ng_ref, w_ref, qg_ref, kvg_ref, wuq_ref, wk_ref, wv_ref,
     mu_ref, w0_ref, w2_ref, a0_ref, a2_ref, kk_ref, ka_ref, rk_ref, ones_ref) = refs[:20]
    n_in = 20 + (2 if rope else 0)
    if rope:
        cos_ref, sin_ref = refs[20:22]
    outs = refs[n_in:]
    i = pl.program_id(1)
    n = pl.num_programs(1)
    tm = x_ref.shape[1]

    x_ext = jnp.concatenate([xp_ref[0], x_ref[0], xn_ref[0]], axis=0)
    h = (_rms(x_ext, ng_ref[...]) * (1.0 + mod_ref[0, 1:2, :]) + mod_ref[0, 0:1, :]).astype(BF16)
    u_all = _dot(h, w_ref[...])
    u_ext = u_all[:, :N_SHIFT]
    rest = u_all[HALO:HALO + tm, N_SHIFT:]
    o0 = 0
    g_rw = rest[:, o0:o0 + RW_WIDTH]; o0 += RW_WIDTH
    g_mla = rest[:, o0:o0 + MLA_WIDTH]; o0 += MLA_WIDTH
    cq = rest[:, o0:o0 + Q_LORA]; o0 += Q_LORA
    ckv = rest[:, o0:o0 + KV_LORA]; o0 += KV_LORA
    kr = rest[:, o0:o0 + LANES]

    ckv_n = _rms(ckv, kvg_ref[...]).astype(BF16)
    k = _dot(ckv_n, wk_ref[...])
    vt = _dot(wv_ref[...], ckv_n, NT)
    if rope:
        cos = cos_ref[...]
        sins = sin_ref[...]
        kr = _rope(kr, cos, sins, lax.broadcasted_iota(jnp.int32, kr.shape, 1))
    k = k + jnp.concatenate([kr] * MLA_HEADS, axis=1)
    if latent:
        (grw_ref, gmla_ref, q_ref, k_ref, v_ref,
         r_ref, vv_ref, a_ref, lwf_ref, lwb_ref, kf_ref, kb_ref, bf_ref, bb_ref, bon_ref) = outs
        grw_ref[0] = _silu(g_rw).astype(BF16)
        gmla_ref[0] = _silu(g_mla).astype(BF16)
        q = _dot(_rms(cq, qg_ref[...]).astype(BF16), wuq_ref[...])
        if rope:
            q = _rope(q, cos, sins, lax.broadcasted_iota(jnp.int32, q.shape, 1))
        q_ref[0] = (q * q_scale).astype(BF16)
    else:
        k_ref, v_ref, r_ref, vv_ref, a_ref, lwf_ref, lwb_ref, kf_ref, kb_ref, bf_ref, bb_ref = outs
    k_ref[0] = k.astype(BF16)
    v_ref[0, 0] = vt.astype(BF16)

    u = u_ext[HALO:HALO + tm]
    rowi = lax.broadcasted_iota(jnp.int32, u.shape, 0)
    u_prev = pltpu.roll(u_ext, 1, axis=0)[HALO:HALO + tm]
    u_next = pltpu.roll(u_ext, tm + 2 * HALO - 1, axis=0)[HALO:HALO + tm]
    u_prev = jnp.where(jnp.logical_and(rowi == 0, i == 0), 0.0, u_prev)
    u_next = jnp.where(jnp.logical_and(rowi == tm - 1, i == n - 1), 0.0, u_next)
    us = u + mu_ref[0:1, :] * (u_prev - u) + mu_ref[1:2, :] * (u_next - u)

    w = RW_WIDTH
    r = us[:, 0:w]
    kx = us[:, w:2 * w]
    v = us[:, 2 * w:3 * w]
    w_in = us[:, 3 * w:3 * w + LANES]
    a_in = us[:, 3 * w + LANES:3 * w + 2 * LANES]
    z = _mm(jnp.tanh(w_in), w2_ref[...], 1) + w0_ref[...]
    lw = -math.exp(-0.5) * _sigmoid(z)
    a_sig = _sigmoid(_mm(a_in, a2_ref[...], 1) + a0_ref[...])
    ones = ones_ref[...]
    kkf = kx * kk_ref[...]
    kk = kkf * lax.rsqrt(jnp.maximum(_mm_exact_rhs(kkf * kkf, ones, 1), 1e-24))
    ka = ka_ref[...]
    k_f = kx * (1.0 + (a_sig[:, :w] - 1.0) * ka)
    k_b = kx * (1.0 + (a_sig[:, w:] - 1.0) * ka)
    r_ref[0] = r.astype(BF16)
    vv_ref[0] = v.astype(BF16)
    a_ref[0] = (-kk).astype(BF16)
    lwf_ref[0] = lw[:, :w]
    lwb_ref[0] = lw[:, w:]
    kf_ref[0] = k_f.astype(BF16)
    kb_ref[0] = k_b.astype(BF16)
    bf_ref[0] = (kk * a_sig[:, :w]).astype(BF16)
    bb_ref[0] = (kk * a_sig[:, w:]).astype(BF16)
    if latent:
        bon_ref[0] = (_mm_exact_rhs(r * (k_f + k_b) * rk_ref[...], ones, 2) * v).astype(BF16)


def _front_call(x, mod3, consts, cos_t, sin_t, *, rope, latent, tm):
    b, l, d = x.shape
    t8 = tm // HALO
    nb8 = l // HALO
    const = lambda bb, i: (0, 0)
    in_specs = [pl.BlockSpec((1, tm, d), lambda bb, i: (bb, i, 0)),
                pl.BlockSpec((1, HALO, d), lambda bb, i: (bb, jnp.maximum(i * t8 - 1, 0), 0)),
                pl.BlockSpec((1, HALO, d), lambda bb, i: (bb, jnp.minimum((i + 1) * t8, nb8 - 1), 0)),
                pl.BlockSpec((1, 3, d), lambda bb, i: (bb, 0, 0))]
    in_specs += [pl.BlockSpec(c.shape, const) for c in consts]
    args = [x, x, x, mod3] + list(consts)
    if rope:
        in_specs += [pl.BlockSpec((tm, LANES), lambda bb, i: (i, 0))] * 2
        args += [cos_t, sin_t]
    row = lambda wd: pl.BlockSpec((1, tm, wd), lambda bb, i: (bb, i, 0))
    sds = lambda wd, dt: jax.ShapeDtypeStruct((b, l, wd), dt)
    kw = MLA_HEADS * QK_PAD
    kv_specs = [row(kw), pl.BlockSpec((1, 1, MLA_WIDTH, tm), lambda bb, i: (bb, i, 0, 0))]
    kv_shapes = [sds(kw, BF16), jax.ShapeDtypeStruct((b, l // tm, MLA_WIDTH, tm), BF16)]
    rw = RW_WIDTH
    scan_dt = [BF16, BF16, BF16, F32, F32, BF16, BF16, BF16, BF16]
    out_specs = kv_specs + [row(rw)] * 9
    out_shape = kv_shapes + [sds(rw, dt) for dt in scan_dt]
    if latent:
        out_specs = [row(rw), row(MLA_WIDTH), row(kw)] + out_specs + [row(rw)]
        out_shape = [sds(rw, BF16), sds(MLA_WIDTH, BF16), sds(kw, BF16)] + out_shape + [sds(rw, BF16)]
    kern = functools.partial(_front_kernel, rope=rope, latent=latent,
                             q_scale=ATTN_SCALE * math.log2(math.e))
    return pl.pallas_call(
        kern, grid=(b, l // tm), in_specs=in_specs, out_specs=out_specs, out_shape=out_shape,
        compiler_params=_params(("parallel", "arbitrary")),
        name="front_lat" if latent else "front_ctx",
    )(*args)


QUAD = 4 * RW_HEAD
_DONE = object()


def _stack_heads(x, masks):
    zero = jnp.zeros_like(x)
    return jnp.concatenate([jnp.where(m, x, zero) for m in masks], axis=0)


def _unstack_heads(x, c):
    return (x[0:c] + x[c:2 * c]) + (x[2 * c:3 * c] + x[3 * c:4 * c])


def _scan_chain(r, kd, v, a, bd, lw, sv_ref, y_ref, rev, consts):
    tri_bf, strict, incl, eye, masks, prow, pcol = consts
    c = r.shape[0]
    g = _mm_exact_lhs(tri_bf, lw, 3)
    g_end = g[0:1, :] if rev else g[c - 1:c, :]
    eng = jnp.exp(-g)
    e_end = jnp.exp(g_end - g)
    rt = r * jnp.exp(g)
    stk = lambda t: _stack_heads(t.astype(BF16), masks)
    at_s = stk(a * jnp.exp(g - lw))
    rt_s = stk(rt)
    bt_s = stk(bd * eng)
    kt_s = stk(kd * eng)
    bh_s = stk(bd * e_end)
    kh_s = stk(kd * e_end)
    v_s = stk(v)

    yield
    a_ab = jnp.where(strict, _dot(at_s, bt_s, NT), 0.0)
    yield
    a_ak = jnp.where(strict, _dot(at_s, kt_s, NT), 0.0).astype(BF16)
    yield
    a_rb = jnp.where(incl, _dot(rt_s, bt_s, NT), 0.0).astype(BF16)
    yield
    a_rk = jnp.where(incl, _dot(rt_s, kt_s, NT), 0.0).astype(BF16)
    blk = lambda sz: (prow // sz) == (pcol // sz)
    lb = a_ab.astype(BF16)
    zero = jnp.zeros_like(lb)
    t_inv = eye + jnp.where(blk(2), a_ab, 0.0)
    sz = 2
    while sz < c:
        off = jnp.where(jnp.logical_and(blk(2 * sz), jnp.logical_not(blk(sz))), lb, zero)
        tb = t_inv.astype(BF16)
        yield
        lt = _dot(off, tb).astype(BF16)
        yield
        t_inv = t_inv + _dot(tb, lt)
        sz *= 2
    tb = t_inv.astype(BF16)
    yield
    akv = _dot(a_ak, v_s).astype(BF16)
    yield
    wu = _dot(tb, jnp.concatenate([at_s, akv], axis=1)).astype(BF16)
    yield
    ry = _dot(a_rb, wu)
    rq = (rt + _unstack_heads(ry[:, :QUAD], c)).astype(BF16)
    yield
    y0 = _unstack_heads(ry[:, QUAD:] + _dot(a_rk, v_s), c)
    sv = sv_ref[...]
    sv_b = sv.astype(BF16)
    yield
    y_ref[...] = _dot(rq, sv_b, NT) + y0
    yield
    sw = _dot(sv_b, wu[:, :QUAD], NT).astype(BF16)
    yield
    upd = _dot(jnp.concatenate([wu[:, QUAD:], v_s], axis=0), jnp.concatenate([bh_s, kh_s], axis=0), TN)
    yield
    sv_ref[...] = sv * jnp.exp(g_end) + _dot(sw, bh_s) + upd


def _scan_kernel(rf_ref, vf_ref, af_ref, lwf_ref, kf_ref, bf_ref,
                 rb_ref, vb_ref, ab_ref, lwb_ref, kb_ref, bb_ref, s0_ref,
                 yf_ref, yb_ref, sout_ref, s_scr):
    i = pl.program_id(0)
    n = pl.num_programs(0)
    nb, c = rf_ref.shape[0], rf_ref.shape[1]
    nquad = rf_ref.shape[2] // QUAD

    @pl.when(i == 0)
    def _():
        s_scr[...] = s0_ref[...]

    row = lax.broadcasted_iota(jnp.int32, (c, c), 0)
    col = lax.broadcasted_iota(jnp.int32, (c, c), 1)
    prow = lax.broadcasted_iota(jnp.int32, (QUAD, QUAD), 0)
    pcol = lax.broadcasted_iota(jnp.int32, (QUAD, QUAD), 1)
    eye = (prow == pcol).astype(F32)
    lane = lax.broadcasted_iota(jnp.int32, (c, QUAD), 1)
    masks = [(lane // RW_HEAD) == j for j in range(QUAD // RW_HEAD)]

    chains = []
    for d, rev, refs, y_ref in ((0, False, (rf_ref, kf_ref, vf_ref, af_ref, bf_ref, lwf_ref), yf_ref),
                                (1, True, (rb_ref, kb_ref, vb_ref, ab_ref, bb_ref, lwb_ref), yb_ref)):
        if rev:
            tri, strict, incl = (col >= row), (pcol > prow), (pcol >= prow)
        else:
            tri, strict, incl = (col <= row), (pcol < prow), (pcol <= prow)
        consts = (tri.astype(BF16), strict, incl, eye, masks, prow, pcol)
        for bi in range(nb):
            for q in range(nquad):
                ins = (x[bi, :, q * QUAD:(q + 1) * QUAD].astype(F32) for x in refs)
                chains.append(_scan_chain(*ins, s_scr.at[bi, d, q], y_ref.at[bi, :, q * QUAD:(q + 1) * QUAD],
                                          rev, consts))
    while chains:
        for ch in list(chains):
            if next(ch, _DONE) is _DONE:
                chains.remove(ch)

    @pl.when(i == n - 1)
    def _():
        sout_ref[...] = s_scr[...]


def _scan_call(r, v, a, lw_f, lw_b, k_f, k_b, b_f, b_b, s0):
    b, l, w = r.shape
    nc = l // CHUNK
    fwd = pl.BlockSpec((b, CHUNK, w), lambda i: (0, i, 0))
    bwd = pl.BlockSpec((b, CHUNK, w), lambda i: (0, nc - 1 - i, 0))
    st = pl.BlockSpec(s0.shape, lambda i: (0, 0, 0, 0, 0))
    sds = jax.ShapeDtypeStruct((b, l, w), F32)
    return pl.pallas_call(
        _scan_kernel, grid=(nc,),
        in_specs=[fwd] * 6 + [bwd] * 6 + [st],
        out_specs=[fwd, bwd, st],
        out_shape=[sds, sds, jax.ShapeDtypeStruct(s0.shape, F32)],
        scratch_shapes=[pltpu.VMEM(s0.shape, F32)],
        compiler_params=_params(("arbitrary",)),
        name="scan",
    )(r, v, a, lw_f, k_f, b_f, r, v, a, lw_b, k_b, b_b, s0)


ONES_ROWS = 16
QBLK = 256


def _attn_kernel(zero_ref, q_ref, k_ref, kc_ref, vt_ref, vtc_ref, o_ref, sa_scr, sb_scr, *, r):
    tq = q_ref.shape[1]
    tv = vt_ref.shape[3]
    tk = r * tv
    n_lat = vt_ref.shape[1]
    nk = (n_lat + vtc_ref.shape[1]) // r
    qs = (q_ref[0, :, 0:QK_PAD], q_ref[0, :, QK_PAD:2 * QK_PAD])
    ones = jnp.ones((ONES_ROWS, tk), BF16)
    z0 = pl.multiple_of(zero_ref[0] * 8, 8)

    qb = min(QBLK, tq)
    nqb = tq // qb
    heads, blocks = range(2), range(nqb)

    def keys(j):
        lo, hi = j * r, (j + 1) * r
        parts = []
        if lo < n_lat:
            parts.append(k_ref[0, lo * tv:min(hi, n_lat) * tv, :])
        if hi > n_lat:
            parts.append(kc_ref[0, (max(lo, n_lat) - n_lat) * tv:(hi - n_lat) * tv, :])
        kt = jnp.concatenate(parts, axis=0) if len(parts) > 1 else parts[0]
        return [kt[:, h * QK_PAD:(h + 1) * QK_PAD] for h in heads]

    def value_tile(idx, h):
        ref, i = (vt_ref, idx) if idx < n_lat else (vtc_ref, idx - n_lat)
        return ref[0, i, h * V_HEAD:(h + 1) * V_HEAD, :]

    def produce(kt, s_scr, h, n):
        s = _dot(kt[h], qs[h][n * qb:(n + 1) * qb], NT)
        s_scr[h, pl.ds(z0, tk), n * qb:(n + 1) * qb] = s
        return jnp.max(s, axis=0, keepdims=True)

    def consume(vaug, s_scr, h, n, cmax, m, acc):
        mn = jnp.maximum(m, cmax)
        p = jnp.exp2(s_scr[h, pl.ds(z0, tk), n * qb:(n + 1) * qb] - mn).astype(BF16)
        return mn, jnp.exp2(m - mn) * acc + _dot(vaug, p)

    neg = jnp.full((1, qb), -1e30, F32)
    zacc = jnp.zeros((V_HEAD + ONES_ROWS, qb), F32)
    m = [[neg for _ in blocks] for _ in heads]
    acc = [[zacc for _ in blocks] for _ in heads]
    bufs = (sa_scr, sb_scr)
    kt = keys(0)
    cmax = [[produce(kt, bufs[0], h, n) for n in blocks] for h in heads]
    for j in range(nk):
        vaug = []
        for h in heads:
            vt = [value_tile(j * r + t, h) for t in range(r)]
            vaug.append(jnp.concatenate([jnp.concatenate(vt, axis=1) if r > 1 else vt[0], ones], axis=0))
        kt = keys(j + 1) if j + 1 < nk else None
        cmax_next = [[None for _ in blocks] for _ in heads]
        for n in blocks:
            for h in heads:
                if kt is not None:
                    cmax_next[h][n] = produce(kt, bufs[(j + 1) % 2], h, n)
                m[h][n], acc[h][n] = consume(vaug[h], bufs[j % 2], h, n, cmax[h][n], m[h][n], acc[h][n])
        cmax = cmax_next
    ot = jnp.concatenate([jnp.concatenate([a[0:V_HEAD] / a[V_HEAD:V_HEAD + 1] for a in acc[h]], axis=1)
                          for h in heads], axis=0)
    o_ref[0] = ot.T.astype(o_ref.dtype)


def _attn_call(q, k, kc, vt, vtc, *, tq, r):
    b, l, _ = q.shape
    tv = vt.shape[3]
    npair = MLA_HEADS // 2
    return pl.pallas_call(
        functools.partial(_attn_kernel, r=r),
        grid=(b, npair, l // tq),
        in_specs=[pl.BlockSpec(memory_space=pltpu.SMEM),
                  pl.BlockSpec((1, tq, 2 * QK_PAD), lambda bb, p, i: (bb, i, p)),
                  pl.BlockSpec((1, k.shape[1], 2 * QK_PAD), lambda bb, p, i: (bb, 0, p)),
                  pl.BlockSpec((1, kc.shape[1], 2 * QK_PAD), lambda bb, p, i: (bb, 0, p)),
                  pl.BlockSpec((1, vt.shape[1], LANES, tv), lambda bb, p, i: (bb, 0, p, 0)),
                  pl.BlockSpec((1, vtc.shape[1], LANES, tv), lambda bb, p, i: (bb, 0, p, 0))],
        out_specs=pl.BlockSpec((1, tq, LANES), lambda bb, p, i: (bb, i, p)),
        out_shape=jax.ShapeDtypeStruct((b, l, MLA_WIDTH), BF16),
        scratch_shapes=[pltpu.VMEM((2, r * tv, tq), F32), pltpu.VMEM((2, r * tv, tq), F32)],
        compiler_params=_params(("parallel", "parallel", "arbitrary")),
        name="attn",
    )(jnp.zeros((1,), jnp.int32), q, k, kc, vt, vtc)


def _outproj_kernel(x_ref, mod_ref, yf_ref, yb_ref, bon_ref, grw_ref, o_ref, gmla_ref, lg_ref, lb_ref, ones_ref,
                    wo_ref, fg_ref, out_ref):
    x = x_ref[0]
    gate = mod_ref[0, 2:3, :]
    ones = ones_ref[...]
    y = yf_ref[0] + yb_ref[0]
    inv_n = 1.0 / RW_HEAD
    mean = _mm_exact_rhs(y, ones, 2) * inv_n
    yc = y - mean
    var = _mm_exact_rhs(yc * yc, ones, 1) * inv_n
    yn = yc * lax.rsqrt(var + LNX_EPS) * lg_ref[...] + lb_ref[...]
    rw = (yn + bon_ref[0].astype(F32)) * grw_ref[0].astype(F32)
    mla = o_ref[0].astype(F32) * gmla_ref[0].astype(F32)
    cat = jnp.concatenate([rw, mla], axis=1).astype(BF16)
    z = x + gate * _dot(cat, wo_ref[...])
    out_ref[0] = z * lax.rsqrt(jnp.mean(z * z, axis=-1, keepdims=True) + NORM_EPS) * fg_ref[...]


def _outproj_call(x, mod3, y_f, y_b, bonus, sg_rw, o, sg_mla, lnx_g, lnx_b, ones_blk, w_out, final_g, *, tm):
    b, l, d = x.shape
    const = lambda bb, i: (0, 0)
    row = lambda w: pl.BlockSpec((1, tm, w), lambda bb, i: (bb, i, 0))
    return pl.pallas_call(
        _outproj_kernel, grid=(b, l // tm),
        in_specs=[row(d), pl.BlockSpec((1, 3, d), lambda bb, i: (bb, 0, 0)),
                  row(RW_WIDTH), row(RW_WIDTH), row(RW_WIDTH), row(RW_WIDTH), row(MLA_WIDTH), row(MLA_WIDTH),
                  pl.BlockSpec(lnx_g.shape, const), pl.BlockSpec(lnx_b.shape, const),
                  pl.BlockSpec(ones_blk.shape, const), pl.BlockSpec(w_out.shape, const),
                  pl.BlockSpec(final_g.shape, const)],
        out_specs=row(d),
        out_shape=jax.ShapeDtypeStruct((b, l, d), F32),
        compiler_params=_params(("parallel", "arbitrary")),
        name="outproj",
    )(x, mod3, y_f, y_b, bonus, sg_rw, o, sg_mla, lnx_g, lnx_b, ones_blk, w_out, final_g)


def _rope_tables(l):
    rows = l // GRID_W
    inv_freq = ROPE_THETA ** (-jnp.arange(0, AXIS_DIM, 2, dtype=F32) / AXIS_DIM)
    ang_r = jnp.arange(rows, dtype=F32)[:, None] * inv_freq
    ang_c = jnp.arange(GRID_W, dtype=F32)[:, None] * inv_freq

    half = AXIS_DIM // 2
    cos_r = jnp.concatenate([jnp.ones((rows, QK_NOPE), F32), jnp.cos(ang_r), jnp.cos(ang_r),
                             jnp.ones((rows, LANES - QK_NOPE - 2 * half), F32)], axis=1)
    cos_c = jnp.concatenate([jnp.ones((GRID_W, QK_NOPE + 2 * half), F32), jnp.cos(ang_c), jnp.cos(ang_c),
                             jnp.ones((GRID_W, LANES - QK_DIM), F32)], axis=1)
    sin_r = jnp.concatenate([jnp.zeros((rows, QK_NOPE), F32), -jnp.sin(ang_r), jnp.sin(ang_r),
                             jnp.zeros((rows, LANES - QK_NOPE - 2 * half), F32)], axis=1)
    sin_c = jnp.concatenate([jnp.zeros((GRID_W, QK_NOPE + 2 * half), F32), -jnp.sin(ang_c), jnp.sin(ang_c),
                             jnp.zeros((GRID_W, LANES - QK_DIM), F32)], axis=1)
    cos_t = (cos_r[:, None, :] * cos_c[None, :, :]).reshape(l, LANES)
    sin_t = (sin_r[:, None, :] + sin_c[None, :, :]).reshape(l, LANES)
    return cos_t, sin_t


def kernel(x, c, ctx, c_ctx, ada_w, ada_b, norm_g, w_in, shift_mu, rw_w0, rw_w2, rw_a0, rw_a2, rw_kk, rw_ka, rw_rk,
           rw_lnx_g, rw_lnx_b, mla_q_norm_g, mla_kv_norm_g, mla_w_uq, mla_w_ukv, w_out, final_g):
    assert x.shape[-1] == 2 * RW_WIDTH and w_in.shape[0] == 1, "single-layer block with d_model = 1024"
    b, l, d = x.shape
    lc = ctx.shape[1]

    w = w_in[0]
    o1 = N_SHIFT + RW_WIDTH
    o2 = o1 + Q_LORA + KV_LORA + QK_ROPE
    w_kr = jnp.zeros((d, LANES), F32).at[:, QK_NOPE:QK_DIM].set(w[:, o1 + Q_LORA + KV_LORA:o2])
    w_cat = jnp.concatenate([w[:, :N_SHIFT], w[:, N_SHIFT:o1], w[:, o2:], w[:, o1:o1 + Q_LORA],
                             w[:, o1 + Q_LORA:o1 + Q_LORA + KV_LORA], w_kr], axis=1).astype(BF16)
    wuq = jnp.pad(mla_w_uq[0].reshape(Q_LORA, MLA_HEADS, QK_DIM), ((0, 0), (0, 0), (0, QK_PAD - QK_DIM)))
    wuq = wuq.reshape(Q_LORA, MLA_HEADS * QK_PAD).astype(BF16)
    wukv = mla_w_ukv[0].reshape(KV_LORA, MLA_HEADS, QK_NOPE + V_HEAD)
    wk = jnp.pad(wukv[:, :, :QK_NOPE], ((0, 0), (0, 0), (0, QK_PAD - QK_NOPE))).reshape(KV_LORA, MLA_HEADS * QK_PAD).astype(BF16)
    wv = wukv[:, :, QK_NOPE:].reshape(KV_LORA, MLA_WIDTH).T.astype(BF16)
    zl = jnp.zeros((DECAY_LORA, RW_WIDTH), F32)
    w2bd = jnp.concatenate([jnp.concatenate([rw_w2[0, 0], zl], axis=1), jnp.concatenate([zl, rw_w2[0, 1]], axis=1)], axis=0)
    a2bd = jnp.concatenate([jnp.concatenate([rw_a2[0, 0], zl], axis=1), jnp.concatenate([zl, rw_a2[0, 1]], axis=1)], axis=0)
    w0 = rw_w0[0].reshape(1, 2 * RW_WIDTH)
    a0 = rw_a0[0].reshape(1, 2 * RW_WIDTH)
    hid = np.arange(RW_WIDTH) // RW_HEAD
    ones_blk = jnp.asarray(hid[:, None] == hid[None, :], BF16)
    cos_t, sin_t = _rope_tables(l)
    row1 = lambda t: t.reshape(1, -1)

    c_rows = jnp.zeros((8, d), F32).at[:b].set(c).at[b].set(c_ctx)
    mod = _mod_call(c_rows, ada_w[0], ada_b[0])
    mod_lat = mod[:b].reshape(b, 3, d)
    mod_ctx = jnp.broadcast_to(mod[b].reshape(1, 3, d), (b, 3, d))

    consts = (row1(norm_g[0]), w_cat, row1(mla_q_norm_g[0]), row1(mla_kv_norm_g[0]), wuq, wk, wv,
              shift_mu[0], w0, w2bd, a0, a2bd, row1(rw_kk[0]), row1(rw_ka[0]), row1(rw_rk[0]), ones_blk)
    tm_f = min(256, l, lc)
    front = functools.partial(_front_call, consts=consts, cos_t=cos_t, sin_t=sin_t, tm=tm_f)
    sg_rw, sg_mla, q, k_lat, vt_lat, *pz = front(x, mod_lat, rope=True, latent=True)
    k_ctx, vt_ctx, *pc = front(ctx, mod_ctx, rope=False, latent=False)

    s_zero = jnp.zeros((b, 2, RW_WIDTH // QUAD, QUAD, QUAD), F32)
    _, _, s_ctx = _scan_call(*pc[:9], s_zero)
    y_f, y_b, _ = _scan_call(*pz[:9], s_ctx)

    r = next(t for t in (3, 2, 1) if (vt_lat.shape[1] + vt_ctx.shape[1]) % t == 0)
    tq = next(t for t in (1024, 512, 256, l) if l % t == 0)
    o = _attn_call(q, k_lat, k_ctx, vt_lat, vt_ctx, tq=tq, r=r)

    return _outproj_call(x, mod_lat, y_f, y_b, pz[9], sg_rw, o, sg_mla, row1(rw_lnx_g[0]), row1(rw_lnx_b[0]),
                         ones_blk, w_out[0].astype(BF16), row1(final_g), tm=min(256, l))
```

```python
import functools
import math

import numpy as np
import jax
import jax.numpy as jnp
from jax import lax
from jax.experimental import pallas as pl
from jax.experimental.pallas import tpu as pltpu

F32 = jnp.float32
BF16 = jnp.bfloat16

RW_HEAD = 64
RW_HEADS = 8
RW_WIDTH = RW_HEAD * RW_HEADS
DECAY_LORA = 64
AAA_LORA = 64
MLA_HEADS = 8
QK_NOPE = 64
QK_ROPE = 32
QK_DIM = QK_NOPE + QK_ROPE
V_HEAD = 64
MLA_WIDTH = MLA_HEADS * V_HEAD
Q_LORA = 384
KV_LORA = 256
AXIS_DIM = QK_ROPE // 2
ROPE_THETA = 10000.0
GRID_W = 64
NORM_EPS = 1e-6
LNX_EPS = 64e-5
ATTN_SCALE = QK_DIM ** -0.5
N_SHIFT = 3 * RW_WIDTH + 2 * DECAY_LORA + 2 * AAA_LORA

LANES = 128
QK_PAD = LANES
CHUNK = 64
VMEM_LIMIT = 48 * 1024 * 1024

NN = (((1,), (0,)), ((), ()))
NT = (((1,), (1,)), ((), ()))
TN = (((0,), (0,)), ((), ()))


def _dot(a, b, dims=NN):
    return lax.dot_general(a, b, dims, preferred_element_type=F32)


def _split2(x):
    hi = x.astype(BF16)
    lo = (x - hi.astype(F32)).astype(BF16)
    return hi, lo


def _split3(x):
    hi = x.astype(BF16)
    r1 = x - hi.astype(F32)
    mid = r1.astype(BF16)
    lo = (r1 - mid.astype(F32)).astype(BF16)
    return hi, mid, lo


def _mm(a, b, passes, dims=NN):
    if passes == 1:
        return _dot(a.astype(BF16), b.astype(BF16), dims)
    if passes == 3:
        ah, al = _split2(a)
        bh, bl = _split2(b)
        return _dot(ah, bh, dims) + (_dot(ah, bl, dims) + _dot(al, bh, dims))
    ah, am, al = _split3(a)
    bh, bm, bl = _split3(b)
    return (_dot(ah, bh, dims) + (_dot(ah, bm, dims) + _dot(am, bh, dims))
            + (_dot(ah, bl, dims) + _dot(al, bh, dims) + _dot(am, bm, dims)))


def _mm_exact_rhs(a, b_bf16, nsplit, dims=NN):
    if nsplit == 1:
        return _dot(a.astype(BF16), b_bf16, dims)
    parts = _split2(a) if nsplit == 2 else _split3(a)
    out = _dot(parts[0], b_bf16, dims)
    for p in parts[1:]:
        out = out + _dot(p, b_bf16, dims)
    return out


def _mm_exact_lhs(a_bf16, b, nsplit, dims=NN):
    parts = _split2(b) if nsplit == 2 else _split3(b)
    out = _dot(a_bf16, parts[0], dims)
    for p in parts[1:]:
        out = out + _dot(a_bf16, p, dims)
    return out


def _sigmoid(x):
    return 1.0 / (1.0 + jnp.exp(-x))


def _silu(x):
    return x * _sigmoid(x)


def _params(sem):
    return pltpu.CompilerParams(dimension_semantics=sem, vmem_limit_bytes=VMEM_LIMIT)


def _mod_kernel(c_ref, w_ref, b_ref, o_ref):
    s = _silu(c_ref[...])
    o_ref[...] = _mm(s, w_ref[...], 6) + b_ref[...]


def _mod_call(c_rows, ada_w, ada_b):
    n, d = c_rows.shape
    d3 = ada_w.shape[1]
    tn = 512
    return pl.pallas_call(
        _mod_kernel,
        grid=(d3 // tn,),
        in_specs=[pl.BlockSpec((n, d), lambda j: (0, 0)),
                  pl.BlockSpec((d, tn), lambda j: (0, j)),
                  pl.BlockSpec((1, tn), lambda j: (0, j))],
        out_specs=pl.BlockSpec((n, tn), lambda j: (0, j)),
        out_shape=jax.ShapeDtypeStruct((n, d3), F32),
        compiler_params=_params(("arbitrary",)),
        name="mod",
    )(c_rows, ada_w, ada_b.reshape(1, d3))


HALO = 8


def _rope(t, cos, sins, lane):
    reps = t.shape[1] // LANES
    cos_f = jnp.concatenate([cos] * reps, axis=1) if reps > 1 else cos
    sin_f = jnp.concatenate([sins] * reps, axis=1) if reps > 1 else sins
    n = t.shape[1]
    lo_half = ((lane % LANES) % (AXIS_DIM)) < (AXIS_DIM // 2)
    partner = jnp.where(lo_half, pltpu.roll(t, n - AXIS_DIM // 2, axis=1), pltpu.roll(t, AXIS_DIM // 2, axis=1))
    return t * cos_f + partner * sin_f


def _rms(t, g):
    return t * lax.rsqrt(jnp.mean(t * t, axis=-1, keepdims=True) + NORM_EPS) * g


def _front_kernel(*refs, rope, latent, q_scale):
    (x_ref, xp_ref, xn_ref, mod_ref, ng_ref, w_ref, qg_ref, kvg_ref, wuq_ref, wk_ref, wv_ref,
     mu_ref, w0_ref, w2_ref, a0_ref, a2_ref, kk_ref, ka_ref, rk_ref, ones_ref) = refs[:20]
    n_in = 20 + (2 if rope else 0)
    if rope:
        cos_ref, sin_ref = refs[20:22]
    outs = refs[n_in:]
    i = pl.program_id(1)
    n = pl.num_programs(1)
    tm = x_ref.shape[1]

    x_ext = jnp.concatenate([xp_ref[0], x_ref[0], xn_ref[0]], axis=0)
    h = (_rms(x_ext, ng_ref[...]) * (1.0 + mod_ref[0, 1:2, :]) + mod_ref[0, 0:1, :]).astype(BF16)
    u_all = _dot(h, w_ref[...])
    u_ext = u_all[:, :N_SHIFT]
    rest = u_all[HALO:HALO + tm, N_SHIFT:]
    o0 = 0
    g_rw = rest[:, o0:o0 + RW_WIDTH]; o0 += RW_WIDTH
    g_mla = rest[:, o0:o0 + MLA_WIDTH]; o0 += MLA_WIDTH
    cq = rest[:, o0:o0 + Q_LORA]; o0 += Q_LORA
    ckv = rest[:, o0:o0 + KV_LORA]; o0 += KV_LORA
    kr = rest[:, o0:o0 + LANES]

    ckv_n = _rms(ckv, kvg_ref[...]).astype(BF16)
    k = _dot(ckv_n, wk_ref[...])
    vt = _dot(wv_ref[...], ckv_n, NT)
    if rope:
        cos = cos_ref[...]
        sins = sin_ref[...]
        kr = _rope(kr, cos, sins, lax.broadcasted_iota(jnp.int32, kr.shape, 1))
    k = k + jnp.concatenate([kr] * MLA_HEADS, axis=1)
    if latent:
        (grw_ref, gmla_ref, q_ref, k_ref, v_ref,
         r_ref, vv_ref, a_ref, lwf_ref, lwb_ref, kf_ref, kb_ref, bf_ref, bb_ref, bon_ref) = outs
        grw_ref[0] = _silu(g_rw).astype(BF16)
        gmla_ref[0] = _silu(g_mla).astype(BF16)
        q = _dot(_rms(cq, qg_ref[...]).astype(BF16), wuq_ref[...])
        if rope:
            q = _rope(q, cos, sins, lax.broadcasted_iota(jnp.int32, q.shape, 1))
        q_ref[0] = (q * q_scale).astype(BF16)
    else:
        k_ref, v_ref, r_ref, vv_ref, a_ref, lwf_ref, lwb_ref, kf_ref, kb_ref, bf_ref, bb_ref = outs
    k_ref[0] = k.astype(BF16)
    v_ref[0, 0] = vt.astype(BF16)

    u = u_ext[HALO:HALO + tm]
    rowi = lax.broadcasted_iota(jnp.int32, u.shape, 0)
    u_prev = pltpu.roll(u_ext, 1, axis=0)[HALO:HALO + tm]
    u_next = pltpu.roll(u_ext, tm + 2 * HALO - 1, axis=0)[HALO:HALO + tm]
    u_prev = jnp.where(jnp.logical_and(rowi == 0, i == 0), 0.0, u_prev)
    u_next = jnp.where(jnp.logical_and(rowi == tm - 1, i == n - 1), 0.0, u_next)
    us = u + mu_ref[0:1, :] * (u_prev - u) + mu_ref[1:2, :] * (u_next - u)

    w = RW_WIDTH
    r = us[:, 0:w]
    kx = us[:, w:2 * w]
    v = us[:, 2 * w:3 * w]
    w_in = us[:, 3 * w:3 * w + LANES]
    a_in = us[:, 3 * w + LANES:3 * w + 2 * LANES]
    z = _mm(jnp.tanh(w_in), w2_ref[...], 1) + w0_ref[...]
    lw = -math.exp(-0.5) * _sigmoid(z)
    a_sig = _sigmoid(_mm(a_in, a2_ref[...], 1) + a0_ref[...])
    ones = ones_ref[...]
    kkf = kx * kk_ref[...]
    kk = kkf * lax.rsqrt(jnp.maximum(_mm_exact_rhs(kkf * kkf, ones, 1), 1e-24))
    ka = ka_ref[...]
    k_f = kx * (1.0 + (a_sig[:, :w] - 1.0) * ka)
    k_b = kx * (1.0 + (a_sig[:, w:] - 1.0) * ka)
    r_ref[0] = r.astype(BF16)
    vv_ref[0] = v.astype(BF16)
    a_ref[0] = (-kk).astype(BF16)
    lwf_ref[0] = lw[:, :w]
    lwb_ref[0] = lw[:, w:]
    kf_ref[0] = k_f.astype(BF16)
    kb_ref[0] = k_b.astype(BF16)
    bf_ref[0] = (kk * a_sig[:, :w]).astype(BF16)
    bb_ref[0] = (kk * a_sig[:, w:]).astype(BF16)
    if latent:
        bon_ref[0] = (_mm_exact_rhs(r * (k_f + k_b) * rk_ref[...], ones, 2) * v).astype(BF16)


def _front_call(x, mod3, consts, cos_t, sin_t, *, rope, latent, tm):
    b, l, d = x.shape
    t8 = tm // HALO
    nb8 = l // HALO
    const = lambda bb, i: (0, 0)
    in_specs = [pl.BlockSpec((1, tm, d), lambda bb, i: (bb, i, 0)),
                pl.BlockSpec((1, HALO, d), lambda bb, i: (bb, jnp.maximum(i * t8 - 1, 0), 0)),
                pl.BlockSpec((1, HALO, d), lambda bb, i: (bb, jnp.minimum((i + 1) * t8, nb8 - 1), 0)),
                pl.BlockSpec((1, 3, d), lambda bb, i: (bb, 0, 0))]
    in_specs += [pl.BlockSpec(c.shape, const) for c in consts]
    args = [x, x, x, mod3] + list(consts)
    if rope:
        in_specs += [pl.BlockSpec((tm, LANES), lambda bb, i: (i, 0))] * 2
        args += [cos_t, sin_t]
    row = lambda wd: pl.BlockSpec((1, tm, wd), lambda bb, i: (bb, i, 0))
    sds = lambda wd, dt: jax.ShapeDtypeStruct((b, l, wd), dt)
    kw = MLA_HEADS * QK_PAD
    kv_specs = [row(kw), pl.BlockSpec((1, 1, MLA_WIDTH, tm), lambda bb, i: (bb, i, 0, 0))]
    kv_shapes = [sds(kw, BF16), jax.ShapeDtypeStruct((b, l // tm, MLA_WIDTH, tm), BF16)]
    rw = RW_WIDTH
    scan_dt = [BF16, BF16, BF16, F32, F32, BF16, BF16, BF16, BF16]
    out_specs = kv_specs + [row(rw)] * 9
    out_shape = kv_shapes + [sds(rw, dt) for dt in scan_dt]
    if latent:
        out_specs = [row(rw), row(MLA_WIDTH), row(kw)] + out_specs + [row(rw)]
        out_shape = [sds(rw, BF16), sds(MLA_WIDTH, BF16), sds(kw, BF16)] + out_shape + [sds(rw, BF16)]
    kern = functools.partial(_front_kernel, rope=rope, latent=latent,
                             q_scale=ATTN_SCALE * math.log2(math.e))
    return pl.pallas_call(
        kern, grid=(b, l // tm), in_specs=in_specs, out_specs=out_specs, out_shape=out_shape,
        compiler_params=_params(("parallel", "arbitrary")),
        name="front_lat" if latent else "front_ctx",
    )(*args)


QUAD = 4 * RW_HEAD
SCAN_SUB = 4
_DONE = object()


def _stack_heads(x, masks):
    zero = jnp.zeros_like(x)
    return jnp.concatenate([jnp.where(m, x, zero) for m in masks], axis=0)


def _unstack_heads(x, c):
    return (x[0:c] + x[c:2 * c]) + (x[2 * c:3 * c] + x[3 * c:4 * c])


def _scan_chain(r, kd, v, a, bd, lw, sv_ref, y_ref, rev, consts):
    tri_bf, strict, incl, eye, masks, prow, pcol = consts
    c = r.shape[0]
    g = _mm_exact_lhs(tri_bf, lw, 2)
    g_end = g[0:1, :] if rev else g[c - 1:c, :]
    eng = jnp.exp(-g)
    e_end = jnp.exp(g_end - g)
    rt = r * jnp.exp(g)
    stk = lambda t: _stack_heads(t.astype(BF16), masks)
    at_s = stk(a * jnp.exp(g - lw))
    rt_s = stk(rt)
    bt_s = stk(bd * eng)
    kt_s = stk(kd * eng)
    bh_s = stk(bd * e_end)
    kh_s = stk(kd * e_end)
    v_s = stk(v)

    yield
    a_ab = jnp.where(strict, _dot(at_s, bt_s, NT), 0.0)
    yield
    a_ak = jnp.where(strict, _dot(at_s, kt_s, NT), 0.0).astype(BF16)
    yield
    a_rb = jnp.where(incl, _dot(rt_s, bt_s, NT), 0.0).astype(BF16)
    yield
    a_rk = jnp.where(incl, _dot(rt_s, kt_s, NT), 0.0).astype(BF16)
    blk = lambda sz: (prow // sz) == (pcol // sz)
    lb = a_ab.astype(BF16)
    zero = jnp.zeros_like(lb)
    t_inv = eye + jnp.where(blk(2), a_ab, 0.0)
    sz = 2
    while sz < c:
        off = jnp.where(jnp.logical_and(blk(2 * sz), jnp.logical_not(blk(sz))), lb, zero)
        tb = t_inv.astype(BF16)
        yield
        lt = _dot(off, tb).astype(BF16)
        yield
        t_inv = t_inv + _dot(tb, lt)
        sz *= 2
    tb = t_inv.astype(BF16)
    yield
    akv = _dot(a_ak, v_s).astype(BF16)
    yield
    wu = _dot(tb, jnp.concatenate([at_s, akv], axis=1))
    sv = sv_ref[...]
    sv_b = sv.astype(BF16)
    yield
    u_s = (_dot(wu[:, :QUAD].astype(BF16), sv_b, NT) + wu[:, QUAD:]).astype(BF16)
    uv = jnp.concatenate([u_s, v_s], axis=0)
    yield
    ys = _dot(jnp.concatenate([a_rb, a_rk], axis=1), uv)
    yield
    y_ref[...] = _unstack_heads(ys, c) + _dot(rt.astype(BF16), sv_b, NT)
    yield
    sv_ref[...] = sv * jnp.exp(g_end) + _dot(uv, jnp.concatenate([bh_s, kh_s], axis=0), TN)


def _scan_kernel(rf_ref, vf_ref, af_ref, lwf_ref, kf_ref, bf_ref,
                 rb_ref, vb_ref, ab_ref, lwb_ref, kb_ref, bb_ref, s0_ref,
                 yf_ref, yb_ref, sout_ref, s_scr):
    i = pl.program_id(0)
    n = pl.num_programs(0)
    nb, c = rf_ref.shape[0], CHUNK
    nsub = rf_ref.shape[1] // c
    nquad = rf_ref.shape[2] // QUAD

    @pl.when(i == 0)
    def _():
        s_scr[...] = s0_ref[...]

    row = lax.broadcasted_iota(jnp.int32, (c, c), 0)
    col = lax.broadcasted_iota(jnp.int32, (c, c), 1)
    prow = lax.broadcasted_iota(jnp.int32, (QUAD, QUAD), 0)
    pcol = lax.broadcasted_iota(jnp.int32, (QUAD, QUAD), 1)
    eye = (prow == pcol).astype(F32)
    lane = lax.broadcasted_iota(jnp.int32, (c, QUAD), 1)
    masks = [(lane // RW_HEAD) == j for j in range(QUAD // RW_HEAD)]

    dirs = []
    for d, rev, refs, y_ref in ((0, False, (rf_ref, kf_ref, vf_ref, af_ref, bf_ref, lwf_ref), yf_ref),
                                (1, True, (rb_ref, kb_ref, vb_ref, ab_ref, bb_ref, lwb_ref), yb_ref)):
        if rev:
            tri, strict, incl = (col >= row), (pcol > prow), (pcol >= prow)
        else:
            tri, strict, incl = (col <= row), (pcol < prow), (pcol <= prow)
        dirs.append((d, rev, refs, y_ref, (tri.astype(BF16), strict, incl, eye, masks, prow, pcol)))
    for step in range(nsub):
        chains = []
        for d, rev, refs, y_ref, consts in dirs:
            sub = nsub - 1 - step if rev else step
            rows = slice(sub * c, (sub + 1) * c)
            for bi in range(nb):
                for q in range(nquad):
                    lanes = slice(q * QUAD, (q + 1) * QUAD)
                    ins = (x[bi, rows, lanes].astype(F32) for x in refs)
                    chains.append(_scan_chain(*ins, s_scr.at[bi, d, q], y_ref.at[bi, rows, lanes], rev, consts))
        while chains:
            for ch in list(chains):
                if next(ch, _DONE) is _DONE:
                    chains.remove(ch)

    @pl.when(i == n - 1)
    def _():
        sout_ref[...] = s_scr[...]


def _scan_call(r, v, a, lw_f, lw_b, k_f, k_b, b_f, b_b, s0):
    b, l, w = r.shape
    rows = SCAN_SUB * CHUNK if l % (SCAN_SUB * CHUNK) == 0 else CHUNK
    nc = l // rows
    fwd = pl.BlockSpec((b, rows, w), lambda i: (0, i, 0))
    bwd = pl.BlockSpec((b, rows, w), lambda i: (0, nc - 1 - i, 0))
    st = pl.BlockSpec(s0.shape, lambda i: (0, 0, 0, 0, 0))
    sds = jax.ShapeDtypeStruct((b, l, w), F32)
    return pl.pallas_call(
        _scan_kernel, grid=(nc,),
        in_specs=[fwd] * 6 + [bwd] * 6 + [st],
        out_specs=[fwd, bwd, st],
        out_shape=[sds, sds, jax.ShapeDtypeStruct(s0.shape, F32)],
        scratch_shapes=[pltpu.VMEM(s0.shape, F32)],
        compiler_params=_params(("arbitrary",)),
        name="scan",
    )(r, v, a, lw_f, k_f, b_f, r, v, a, lw_b, k_b, b_b, s0)


ONES_ROWS = 16
QBLK = 256


def _attn_kernel(zero_ref, q_ref, k_ref, kc_ref, vt_ref, vtc_ref, o_ref, sa_scr, sb_scr, *, r):
    tq = q_ref.shape[1]
    tv = vt_ref.shape[3]
    tk = r * tv
    n_lat = vt_ref.shape[1]
    nk = (n_lat + vtc_ref.shape[1]) // r
    qs = (q_ref[0, :, 0:QK_PAD], q_ref[0, :, QK_PAD:2 * QK_PAD])
    ones = jnp.ones((ONES_ROWS, tk), BF16)
    z0 = pl.multiple_of(zero_ref[0] * 8, 8)

    qb = min(QBLK, tq)
    nqb = tq // qb
    heads, blocks = range(2), range(nqb)

    def keys(j):
        lo, hi = j * r, (j + 1) * r
        parts = []
        if lo < n_lat:
            parts.append(k_ref[0, lo * tv:min(hi, n_lat) * tv, :])
        if hi > n_lat:
            parts.append(kc_ref[0, (max(lo, n_lat) - n_lat) * tv:(hi - n_lat) * tv, :])
        kt = jnp.concatenate(parts, axis=0) if len(parts) > 1 else parts[0]
        return [kt[:, h * QK_PAD:(h + 1) * QK_PAD] for h in heads]

    def value_tile(idx, h):
        ref, i = (vt_ref, idx) if idx < n_lat else (vtc_ref, idx - n_lat)
        return ref[0, i, h * V_HEAD:(h + 1) * V_HEAD, :]

    def produce(kt, s_scr, h, n):
        s = _dot(kt[h], qs[h][n * qb:(n + 1) * qb], NT)
        s_scr[h, pl.ds(z0, tk), n * qb:(n + 1) * qb] = s
        return jnp.max(s, axis=0, keepdims=True)

    def consume(vaug, s_scr, h, n, cmax, m, acc):
        mn = jnp.maximum(m, cmax)
        p = jnp.exp2(s_scr[h, pl.ds(z0, tk), n * qb:(n + 1) * qb] - mn).astype(BF16)
        return mn, jnp.exp2(m - mn) * acc + _dot(vaug, p)

    neg = jnp.full((1, qb), -1e30, F32)
    zacc = jnp.zeros((V_HEAD + ONES_ROWS, qb), F32)
    m = [[neg for _ in blocks] for _ in heads]
    acc = [[zacc for _ in blocks] for _ in heads]
    bufs = (sa_scr, sb_scr)
    kt = keys(0)
    cmax = [[produce(kt, bufs[0], h, n) for n in blocks] for h in heads]
    for j in range(nk):
        vaug = []
        for h in heads:
            vt = [value_tile(j * r + t, h) for t in range(r)]
            vaug.append(jnp.concatenate([jnp.concatenate(vt, axis=1) if r > 1 else vt[0], ones], axis=0))
        kt = keys(j + 1) if j + 1 < nk else None
        cmax_next = [[None for _ in blocks] for _ in heads]
        for n in blocks:
            for h in heads:
                if kt is not None:
                    cmax_next[h][n] = produce(kt, bufs[(j + 1) % 2], h, n)
                m[h][n], acc[h][n] = consume(vaug[h], bufs[j % 2], h, n, cmax[h][n], m[h][n], acc[h][n])
        cmax = cmax_next
    ot = jnp.concatenate([jnp.concatenate([a[0:V_HEAD] / a[V_HEAD:V_HEAD + 1] for a in acc[h]], axis=1)
                          for h in heads], axis=0)
    o_ref[0] = ot.T.astype(o_ref.dtype)


def _attn_call(q, k, kc, vt, vtc, *, tq, r):
    b, l, _ = q.shape
    tv = vt.shape[3]
    npair = MLA_HEADS // 2
    return pl.pallas_call(
        functools.partial(_attn_kernel, r=r),
        grid=(b, npair, l // tq),
        in_specs=[pl.BlockSpec(memory_space=pltpu.SMEM),
                  pl.BlockSpec((1, tq, 2 * QK_PAD), lambda bb, p, i: (bb, i, p)),
                  pl.BlockSpec((1, k.shape[1], 2 * QK_PAD), lambda bb, p, i: (bb, 0, p)),
                  pl.BlockSpec((1, kc.shape[1], 2 * QK_PAD), lambda bb, p, i: (bb, 0, p)),
                  pl.BlockSpec((1, vt.shape[1], LANES, tv), lambda bb, p, i: (bb, 0, p, 0)),
                  pl.BlockSpec((1, vtc.shape[1], LANES, tv), lambda bb, p, i: (bb, 0, p, 0))],
        out_specs=pl.BlockSpec((1, tq, LANES), lambda bb, p, i: (bb, i, p)),
        out_shape=jax.ShapeDtypeStruct((b, l, MLA_WIDTH), BF16),
        scratch_shapes=[pltpu.VMEM((2, r * tv, tq), F32), pltpu.VMEM((2, r * tv, tq), F32)],
        compiler_params=_params(("parallel", "parallel", "arbitrary")),
        name="attn",
    )(jnp.zeros((1,), jnp.int32), q, k, kc, vt, vtc)


def _outproj_kernel(x_ref, mod_ref, yf_ref, yb_ref, bon_ref, grw_ref, o_ref, gmla_ref, lg_ref, lb_ref, ones_ref,
                    wo_ref, fg_ref, out_ref):
    x = x_ref[0]
    gate = mod_ref[0, 2:3, :]
    ones = ones_ref[...]
    y = yf_ref[0] + yb_ref[0]
    inv_n = 1.0 / RW_HEAD
    mean = _mm_exact_rhs(y, ones, 2) * inv_n
    yc = y - mean
    var = _mm_exact_rhs(yc * yc, ones, 1) * inv_n
    yn = yc * lax.rsqrt(var + LNX_EPS) * lg_ref[...] + lb_ref[...]
    rw = (yn + bon_ref[0].astype(F32)) * grw_ref[0].astype(F32)
    mla = o_ref[0].astype(F32) * gmla_ref[0].astype(F32)
    cat = jnp.concatenate([rw, mla], axis=1).astype(BF16)
    z = x + gate * _dot(cat, wo_ref[...])
    out_ref[0] = z * lax.rsqrt(jnp.mean(z * z, axis=-1, keepdims=True) + NORM_EPS) * fg_ref[...]


def _outproj_call(x, mod3, y_f, y_b, bonus, sg_rw, o, sg_mla, lnx_g, lnx_b, ones_blk, w_out, final_g, *, tm):
    b, l, d = x.shape
    const = lambda bb, i: (0, 0)
    row = lambda w: pl.BlockSpec((1, tm, w), lambda bb, i: (bb, i, 0))
    return pl.pallas_call(
        _outproj_kernel, grid=(b, l // tm),
        in_specs=[row(d), pl.BlockSpec((1, 3, d), lambda bb, i: (bb, 0, 0)),
                  row(RW_WIDTH), row(RW_WIDTH), row(RW_WIDTH), row(RW_WIDTH), row(MLA_WIDTH), row(MLA_WIDTH),
                  pl.BlockSpec(lnx_g.shape, const), pl.BlockSpec(lnx_b.shape, const),
                  pl.BlockSpec(ones_blk.shape, const), pl.BlockSpec(w_out.shape, const),
                  pl.BlockSpec(final_g.shape, const)],
        out_specs=row(d),
        out_shape=jax.ShapeDtypeStruct((b, l, d), F32),
        compiler_params=_params(("parallel", "arbitrary")),
        name="outproj",
    )(x, mod3, y_f, y_b, bonus, sg_rw, o, sg_mla, lnx_g, lnx_b, ones_blk, w_out, final_g)


def _rope_tables(l):
    rows = l // GRID_W
    inv_freq = ROPE_THETA ** (-jnp.arange(0, AXIS_DIM, 2, dtype=F32) / AXIS_DIM)
    ang_r = jnp.arange(rows, dtype=F32)[:, None] * inv_freq
    ang_c = jnp.arange(GRID_W, dtype=F32)[:, None] * inv_freq

    half = AXIS_DIM // 2
    cos_r = jnp.concatenate([jnp.ones((rows, QK_NOPE), F32), jnp.cos(ang_r), jnp.cos(ang_r),
                             jnp.ones((rows, LANES - QK_NOPE - 2 * half), F32)], axis=1)
    cos_c = jnp.concatenate([jnp.ones((GRID_W, QK_NOPE + 2 * half), F32), jnp.cos(ang_c), jnp.cos(ang_c),
                             jnp.ones((GRID_W, LANES - QK_DIM), F32)], axis=1)
    sin_r = jnp.concatenate([jnp.zeros((rows, QK_NOPE), F32), -jnp.sin(ang_r), jnp.sin(ang_r),
                             jnp.zeros((rows, LANES - QK_NOPE - 2 * half), F32)], axis=1)
    sin_c = jnp.concatenate([jnp.zeros((GRID_W, QK_NOPE + 2 * half), F32), -jnp.sin(ang_c), jnp.sin(ang_c),
                             jnp.zeros((GRID_W, LANES - QK_DIM), F32)], axis=1)
    cos_t = (cos_r[:, None, :] * cos_c[None, :, :]).reshape(l, LANES)
    sin_t = (sin_r[:, None, :] + sin_c[None, :, :]).reshape(l, LANES)
    return cos_t, sin_t


def kernel(x, c, ctx, c_ctx, ada_w, ada_b, norm_g, w_in, shift_mu, rw_w0, rw_w2, rw_a0, rw_a2, rw_kk, rw_ka, rw_rk,
           rw_lnx_g, rw_lnx_b, mla_q_norm_g, mla_kv_norm_g, mla_w_uq, mla_w_ukv, w_out, final_g):
    assert x.shape[-1] == 2 * RW_WIDTH and w_in.shape[0] == 1, "single-layer block with d_model = 1024"
    b, l, d = x.shape
    lc = ctx.shape[1]

    w = w_in[0]
    o1 = N_SHIFT + RW_WIDTH
    o2 = o1 + Q_LORA + KV_LORA + QK_ROPE
    w_kr = jnp.zeros((d, LANES), F32).at[:, QK_NOPE:QK_DIM].set(w[:, o1 + Q_LORA + KV_LORA:o2])
    w_cat = jnp.concatenate([w[:, :N_SHIFT], w[:, N_SHIFT:o1], w[:, o2:], w[:, o1:o1 + Q_LORA],
                             w[:, o1 + Q_LORA:o1 + Q_LORA + KV_LORA], w_kr], axis=1).astype(BF16)
    wuq = jnp.pad(mla_w_uq[0].reshape(Q_LORA, MLA_HEADS, QK_DIM), ((0, 0), (0, 0), (0, QK_PAD - QK_DIM)))
    wuq = wuq.reshape(Q_LORA, MLA_HEADS * QK_PAD).astype(BF16)
    wukv = mla_w_ukv[0].reshape(KV_LORA, MLA_HEADS, QK_NOPE + V_HEAD)
    wk = jnp.pad(wukv[:, :, :QK_NOPE], ((0, 0), (0, 0), (0, QK_PAD - QK_NOPE))).reshape(KV_LORA, MLA_HEADS * QK_PAD).astype(BF16)
    wv = wukv[:, :, QK_NOPE:].reshape(KV_LORA, MLA_WIDTH).T.astype(BF16)
    zl = jnp.zeros((DECAY_LORA, RW_WIDTH), F32)
    w2bd = jnp.concatenate([jnp.concatenate([rw_w2[0, 0], zl], axis=1), jnp.concatenate([zl, rw_w2[0, 1]], axis=1)], axis=0)
    a2bd = jnp.concatenate([jnp.concatenate([rw_a2[0, 0], zl], axis=1), jnp.concatenate([zl, rw_a2[0, 1]], axis=1)], axis=0)
    w0 = rw_w0[0].reshape(1, 2 * RW_WIDTH)
    a0 = rw_a0[0].reshape(1, 2 * RW_WIDTH)
    hid = np.arange(RW_WIDTH) // RW_HEAD
    ones_blk = jnp.asarray(hid[:, None] == hid[None, :], BF16)
    cos_t, sin_t = _rope_tables(l)
    row1 = lambda t: t.reshape(1, -1)

    c_rows = jnp.zeros((8, d), F32).at[:b].set(c).at[b].set(c_ctx)
    mod = _mod_call(c_rows, ada_w[0], ada_b[0])
    mod_lat = mod[:b].reshape(b, 3, d)
    mod_ctx = jnp.broadcast_to(mod[b].reshape(1, 3, d), (b, 3, d))

    consts = (row1(norm_g[0]), w_cat, row1(mla_q_norm_g[0]), row1(mla_kv_norm_g[0]), wuq, wk, wv,
              shift_mu[0], w0, w2bd, a0, a2bd, row1(rw_kk[0]), row1(rw_ka[0]), row1(rw_rk[0]), ones_blk)
    tm_f = min(256, l, lc)
    front = functools.partial(_front_call, consts=consts, cos_t=cos_t, sin_t=sin_t, tm=tm_f)
    sg_rw, sg_mla, q, k_lat, vt_lat, *pz = front(x, mod_lat, rope=True, latent=True)
    k_ctx, vt_ctx, *pc = front(ctx, mod_ctx, rope=False, latent=False)

    s_zero = jnp.zeros((b, 2, RW_WIDTH // QUAD, QUAD, QUAD), F32)
    _, _, s_ctx = _scan_call(*pc[:9], s_zero)
    y_f, y_b, _ = _scan_call(*pz[:9], s_ctx)

    r = next(t for t in (3, 2, 1) if (vt_lat.shape[1] + vt_ctx.shape[1]) % t == 0)
    tq = next(t for t in (1024, 512, 256, l) if l % t == 0)
    o = _attn_call(q, k_lat, k_ctx, vt_lat, vt_ctx, tq=tq, r=r)

    return _outproj_call(x, mod_lat, y_f, y_b, pz[9], sg_rw, o, sg_mla, row1(rw_lnx_g[0]), row1(rw_lnx_b[0]),
                         ones_blk, w_out[0].astype(BF16), row1(final_g), tm=min(256, l))
```

```python
import functools
import math

import numpy as np
import jax
import jax.numpy as jnp
from jax import lax
from jax.experimental import pallas as pl
from jax.experimental.pallas import tpu as pltpu

F32 = jnp.float32
BF16 = jnp.bfloat16

RW_HEAD = 64
RW_HEADS = 8
RW_WIDTH = RW_HEAD * RW_HEADS
DECAY_LORA = 64
AAA_LORA = 64
MLA_HEADS = 8
QK_NOPE = 64
QK_ROPE = 32
QK_DIM = QK_NOPE + QK_ROPE
V_HEAD = 64
MLA_WIDTH = MLA_HEADS * V_HEAD
Q_LORA = 384
KV_LORA = 256
AXIS_DIM = QK_ROPE // 2
ROPE_THETA = 10000.0
GRID_W = 64
NORM_EPS = 1e-6
LNX_EPS = 64e-5
ATTN_SCALE = QK_DIM ** -0.5
N_SHIFT = 3 * RW_WIDTH + 2 * DECAY_LORA + 2 * AAA_LORA

LANES = 128
QK_PAD = LANES
CHUNK = 64
VMEM_LIMIT = 48 * 1024 * 1024

NN = (((1,), (0,)), ((), ()))
NT = (((1,), (1,)), ((), ()))
TN = (((0,), (0,)), ((), ()))


def _dot(a, b, dims=NN):
    return lax.dot_general(a, b, dims, preferred_element_type=F32)


def _split2(x):
    hi = x.astype(BF16)
    lo = (x - hi.astype(F32)).astype(BF16)
    return hi, lo


def _split3(x):
    hi = x.astype(BF16)
    r1 = x - hi.astype(F32)
    mid = r1.astype(BF16)
    lo = (r1 - mid.astype(F32)).astype(BF16)
    return hi, mid, lo


def _mm(a, b, passes, dims=NN):
    if passes == 1:
        return _dot(a.astype(BF16), b.astype(BF16), dims)
    if passes == 3:
        ah, al = _split2(a)
        bh, bl = _split2(b)
        return _dot(ah, bh, dims) + (_dot(ah, bl, dims) + _dot(al, bh, dims))
    ah, am, al = _split3(a)
    bh, bm, bl = _split3(b)
    return (_dot(ah, bh, dims) + (_dot(ah, bm, dims) + _dot(am, bh, dims))
            + (_dot(ah, bl, dims) + _dot(al, bh, dims) + _dot(am, bm, dims)))


def _mm_exact_rhs(a, b_bf16, nsplit, dims=NN):
    if nsplit == 1:
        return _dot(a.astype(BF16), b_bf16, dims)
    parts = _split2(a) if nsplit == 2 else _split3(a)
    out = _dot(parts[0], b_bf16, dims)
    for p in parts[1:]:
        out = out + _dot(p, b_bf16, dims)
    return out


def _mm_exact_lhs(a_bf16, b, nsplit, dims=NN):
    parts = _split2(b) if nsplit == 2 else _split3(b)
    out = _dot(a_bf16, parts[0], dims)
    for p in parts[1:]:
        out = out + _dot(a_bf16, p, dims)
    return out


def _sigmoid(x):
    return 1.0 / (1.0 + jnp.exp(-x))


def _silu(x):
    return x * _sigmoid(x)


def _params(sem):
    return pltpu.CompilerParams(dimension_semantics=sem, vmem_limit_bytes=VMEM_LIMIT)


def _mod_kernel(c_ref, w_ref, b_ref, o_ref):
    s = _silu(c_ref[...])
    o_ref[...] = _mm(s, w_ref[...], 6) + b_ref[...]


def _mod_call(c_rows, ada_w, ada_b):
    n, d = c_rows.shape
    d3 = ada_w.shape[1]
    tn = 512
    return pl.pallas_call(
        _mod_kernel,
        grid=(d3 // tn,),
        in_specs=[pl.BlockSpec((n, d), lambda j: (0, 0)),
                  pl.BlockSpec((d, tn), lambda j: (0, j)),
                  pl.BlockSpec((1, tn), lambda j: (0, j))],
        out_specs=pl.BlockSpec((n, tn), lambda j: (0, j)),
        out_shape=jax.ShapeDtypeStruct((n, d3), F32),
        compiler_params=_params(("arbitrary",)),
        name="mod",
    )(c_rows, ada_w, ada_b.reshape(1, d3))


HALO = 8


def _rope(t, cos, sins, lane):
    reps = t.shape[1] // LANES
    cos_f = jnp.concatenate([cos] * reps, axis=1) if reps > 1 else cos
    sin_f = jnp.concatenate([sins] * reps, axis=1) if reps > 1 else sins
    n = t.shape[1]
    lo_half = ((lane % LANES) % (AXIS_DIM)) < (AXIS_DIM // 2)
    partner = jnp.where(lo_half, pltpu.roll(t, n - AXIS_DIM // 2, axis=1), pltpu.roll(t, AXIS_DIM // 2, axis=1))
    return t * cos_f + partner * sin_f


def _rms(t, g):
    return t * lax.rsqrt(jnp.mean(t * t, axis=-1, keepdims=True) + NORM_EPS) * g


def _front_kernel(*refs, rope, latent, q_scale):
    (x_ref, xp_ref, xn_ref, mod_ref, ng_ref, w_ref, qg_ref, kvg_ref, wuq_ref, wk_ref, wv_ref,
     mu_ref, w0_ref, w2_ref, a0_ref, a2_ref, kk_ref, ka_ref, rk_ref, ones_ref) = refs[:20]
    n_in = 20 + (2 if rope else 0)
    if rope:
        cos_ref, sin_ref = refs[20:22]
    outs = refs[n_in:]
    i = pl.program_id(1)
    n = pl.num_programs(1)
    tm = x_ref.shape[1]

    x_ext = jnp.concatenate([xp_ref[0], x_ref[0], xn_ref[0]], axis=0)
    h = (_rms(x_ext, ng_ref[...]) * (1.0 + mod_ref[0, 1:2, :]) + mod_ref[0, 0:1, :]).astype(BF16)
    u_all = _dot(h, w_ref[...])
    u_ext = u_all[:, :N_SHIFT]
    rest = u_all[HALO:HALO + tm, N_SHIFT:]
    o0 = 0
    g_rw = rest[:, o0:o0 + RW_WIDTH]; o0 += RW_WIDTH
    g_mla = rest[:, o0:o0 + MLA_WIDTH]; o0 += MLA_WIDTH
    cq = rest[:, o0:o0 + Q_LORA]; o0 += Q_LORA
    ckv = rest[:, o0:o0 + KV_LORA]; o0 += KV_LORA
    kr = rest[:, o0:o0 + LANES]

    ckv_n = _rms(ckv, kvg_ref[...]).astype(BF16)
    k = _dot(ckv_n, wk_ref[...])
    vt = _dot(wv_ref[...], ckv_n, NT)
    if rope:
        cos = cos_ref[...]
        sins = sin_ref[...]
        kr = _rope(kr, cos, sins, lax.broadcasted_iota(jnp.int32, kr.shape, 1))
    k = k + jnp.concatenate([kr] * MLA_HEADS, axis=1)
    if latent:
        (grw_ref, gmla_ref, q_ref, k_ref, v_ref,
         r_ref, vv_ref, a_ref, lwf_ref, lwb_ref, kf_ref, kb_ref, bf_ref, bb_ref, bon_ref) = outs
        grw_ref[0] = _silu(g_rw).astype(BF16)
        gmla_ref[0] = _silu(g_mla).astype(BF16)
        q = _dot(_rms(cq, qg_ref[...]).astype(BF16), wuq_ref[...])
        if rope:
            q = _rope(q, cos, sins, lax.broadcasted_iota(jnp.int32, q.shape, 1))
        q_ref[0] = (q * q_scale).astype(BF16)
    else:
        k_ref, v_ref, r_ref, vv_ref, a_ref, lwf_ref, lwb_ref, kf_ref, kb_ref, bf_ref, bb_ref = outs
    k_ref[0] = k.astype(BF16)
    v_ref[0, 0] = vt.astype(BF16)

    u = u_ext[HALO:HALO + tm]
    rowi = lax.broadcasted_iota(jnp.int32, u.shape, 0)
    u_prev = pltpu.roll(u_ext, 1, axis=0)[HALO:HALO + tm]
    u_next = pltpu.roll(u_ext, tm + 2 * HALO - 1, axis=0)[HALO:HALO + tm]
    u_prev = jnp.where(jnp.logical_and(rowi == 0, i == 0), 0.0, u_prev)
    u_next = jnp.where(jnp.logical_and(rowi == tm - 1, i == n - 1), 0.0, u_next)
    us = u + mu_ref[0:1, :] * (u_prev - u) + mu_ref[1:2, :] * (u_next - u)

    w = RW_WIDTH
    r = us[:, 0:w]
    kx = us[:, w:2 * w]
    v = us[:, 2 * w:3 * w]
    w_in = us[:, 3 * w:3 * w + LANES]
    a_in = us[:, 3 * w + LANES:3 * w + 2 * LANES]
    z = _mm(jnp.tanh(w_in), w2_ref[...], 1) + w0_ref[...]
    lw = -math.exp(-0.5) * _sigmoid(z)
    a_sig = _sigmoid(_mm(a_in, a2_ref[...], 1) + a0_ref[...])
    ones = ones_ref[...]
    kkf = kx * kk_ref[...]
    kk = kkf * lax.rsqrt(jnp.maximum(_mm_exact_rhs(kkf * kkf, ones, 1), 1e-24))
    ka = ka_ref[...]
    k_f = kx * (1.0 + (a_sig[:, :w] - 1.0) * ka)
    k_b = kx * (1.0 + (a_sig[:, w:] - 1.0) * ka)
    r_ref[0] = r.astype(BF16)
    vv_ref[0] = v.astype(BF16)
    a_ref[0] = (-kk).astype(BF16)
    lwf_ref[0] = lw[:, :w]
    lwb_ref[0] = lw[:, w:]
    kf_ref[0] = k_f.astype(BF16)
    kb_ref[0] = k_b.astype(BF16)
    bf_ref[0] = (kk * a_sig[:, :w]).astype(BF16)
    bb_ref[0] = (kk * a_sig[:, w:]).astype(BF16)
    if latent:
        bon_ref[0] = (_mm_exact_rhs(r * (k_f + k_b) * rk_ref[...], ones, 2) * v).astype(BF16)


def _front_call(x, mod3, consts, cos_t, sin_t, *, rope, latent, tm):
    b, l, d = x.shape
    t8 = tm // HALO
    nb8 = l // HALO
    const = lambda bb, i: (0, 0)
    in_specs = [pl.BlockSpec((1, tm, d), lambda bb, i: (bb, i, 0)),
                pl.BlockSpec((1, HALO, d), lambda bb, i: (bb, jnp.maximum(i * t8 - 1, 0), 0)),
                pl.BlockSpec((1, HALO, d), lambda bb, i: (bb, jnp.minimum((i + 1) * t8, nb8 - 1), 0)),
                pl.BlockSpec((1, 3, d), lambda bb, i: (bb, 0, 0))]
    in_specs += [pl.BlockSpec(c.shape, const) for c in consts]
    args = [x, x, x, mod3] + list(consts)
    if rope:
        in_specs += [pl.BlockSpec((tm, LANES), lambda bb, i: (i, 0))] * 2
        args += [cos_t, sin_t]
    row = lambda wd: pl.BlockSpec((1, tm, wd), lambda bb, i: (bb, i, 0))
    sds = lambda wd, dt: jax.ShapeDtypeStruct((b, l, wd), dt)
    kw = MLA_HEADS * QK_PAD
    kv_specs = [row(kw), pl.BlockSpec((1, 1, MLA_WIDTH, tm), lambda bb, i: (bb, i, 0, 0))]
    kv_shapes = [sds(kw, BF16), jax.ShapeDtypeStruct((b, l // tm, MLA_WIDTH, tm), BF16)]
    rw = RW_WIDTH
    scan_dt = [BF16, BF16, BF16, F32, F32, BF16, BF16, BF16, BF16]
    out_specs = kv_specs + [row(rw)] * 9
    out_shape = kv_shapes + [sds(rw, dt) for dt in scan_dt]
    if latent:
        out_specs = [row(rw), row(MLA_WIDTH), row(kw)] + out_specs + [row(rw)]
        out_shape = [sds(rw, BF16), sds(MLA_WIDTH, BF16), sds(kw, BF16)] + out_shape + [sds(rw, BF16)]
    kern = functools.partial(_front_kernel, rope=rope, latent=latent,
                             q_scale=ATTN_SCALE * math.log2(math.e))
    return pl.pallas_call(
        kern, grid=(b, l // tm), in_specs=in_specs, out_specs=out_specs, out_shape=out_shape,
        compiler_params=_params(("parallel", "arbitrary")),
        name="front_lat" if latent else "front_ctx",
    )(*args)


QUAD = 4 * RW_HEAD
SCAN_SUB = 4
_DONE = object()


def _stack_heads(x, masks):
    zero = jnp.zeros_like(x)
    return jnp.concatenate([jnp.where(m, x, zero) for m in masks], axis=0)


def _unstack_heads(x, c):
    return (x[0:c] + x[c:2 * c]) + (x[2 * c:3 * c] + x[3 * c:4 * c])


def _scan_chain(r, kd, v, a, bd, lw, sv_ref, y_ref, rev, consts):
    tri_bf, strict, incl, eye, masks, prow, pcol = consts
    c = r.shape[0]
    g = _mm_exact_lhs(tri_bf, lw, 2)
    g_end = g[0:1, :] if rev else g[c - 1:c, :]
    eng = jnp.exp(-g)
    e_end = jnp.exp(g_end - g)
    rt = r * jnp.exp(g)
    stk = lambda t: _stack_heads(t.astype(BF16), masks)
    at_s = stk(a * jnp.exp(g - lw))
    rt_s = stk(rt)
    bt_s = stk(bd * eng)
    kt_s = stk(kd * eng)
    bh_s = stk(bd * e_end)
    kh_s = stk(kd * e_end)
    v_s = stk(v)

    ar_s = jnp.concatenate([at_s, rt_s], axis=0)
    n_s = at_s.shape[0]
    yield
    xb = _dot(ar_s, bt_s, NT)
    a_ab = jnp.where(strict, xb[:n_s], 0.0)
    a_rb = jnp.where(incl, xb[n_s:], 0.0).astype(BF16)
    yield
    xk = _dot(ar_s, kt_s, NT)
    a_ak = jnp.where(strict, xk[:n_s], 0.0).astype(BF16)
    a_rk = jnp.where(incl, xk[n_s:], 0.0).astype(BF16)
    blk = lambda sz: (prow // sz) == (pcol // sz)
    lb = a_ab.astype(BF16)
    zero = jnp.zeros_like(lb)
    t_inv = eye + jnp.where(blk(2), a_ab, 0.0)
    sz = 2
    while sz < c:
        off = jnp.where(jnp.logical_and(blk(2 * sz), jnp.logical_not(blk(sz))), lb, zero)
        tb = t_inv.astype(BF16)
        yield
        lt = _dot(off, tb).astype(BF16)
        yield
        t_inv = t_inv + _dot(tb, lt)
        sz *= 2
    tb = t_inv.astype(BF16)
    yield
    akv = _dot(a_ak, v_s).astype(BF16)
    yield
    wu = _dot(tb, jnp.concatenate([at_s, akv], axis=1))
    sv = sv_ref[...]
    sv_b = sv.astype(BF16)
    yield
    u_s = (_dot(wu[:, :QUAD].astype(BF16), sv_b, NT) + wu[:, QUAD:]).astype(BF16)
    uv = jnp.concatenate([u_s, v_s], axis=0)
    yield
    ys = _dot(jnp.concatenate([a_rb, a_rk], axis=1), uv)
    yield
    y_ref[...] = _unstack_heads(ys, c) + _dot(rt.astype(BF16), sv_b, NT)
    yield
    sv_ref[...] = sv * jnp.exp(g_end) + _dot(uv, jnp.concatenate([bh_s, kh_s], axis=0), TN)


def _scan_kernel(rf_ref, vf_ref, af_ref, lwf_ref, kf_ref, bf_ref,
                 rb_ref, vb_ref, ab_ref, lwb_ref, kb_ref, bb_ref, s0_ref,
                 yf_ref, yb_ref, sout_ref, s_scr):
    i = pl.program_id(0)
    n = pl.num_programs(0)
    nb, c = rf_ref.shape[0], CHUNK
    nsub = rf_ref.shape[1] // c
    nquad = rf_ref.shape[2] // QUAD

    @pl.when(i == 0)
    def _():
        s_scr[...] = s0_ref[...]

    row = lax.broadcasted_iota(jnp.int32, (c, c), 0)
    col = lax.broadcasted_iota(jnp.int32, (c, c), 1)
    prow = lax.broadcasted_iota(jnp.int32, (QUAD, QUAD), 0)
    pcol = lax.broadcasted_iota(jnp.int32, (QUAD, QUAD), 1)
    eye = (prow == pcol).astype(F32)
    lane = lax.broadcasted_iota(jnp.int32, (c, QUAD), 1)
    masks = [(lane // RW_HEAD) == j for j in range(QUAD // RW_HEAD)]

    dirs = []
    for d, rev, refs, y_ref in ((0, False, (rf_ref, kf_ref, vf_ref, af_ref, bf_ref, lwf_ref), yf_ref),
                                (1, True, (rb_ref, kb_ref, vb_ref, ab_ref, bb_ref, lwb_ref), yb_ref)):
        if rev:
            tri, strict, incl = (col >= row), (pcol > prow), (pcol >= prow)
        else:
            tri, strict, incl = (col <= row), (pcol < prow), (pcol <= prow)
        dirs.append((d, rev, refs, y_ref, (tri.astype(BF16), strict, incl, eye, masks, prow, pcol)))
    for step in range(nsub):
        chains = []
        for d, rev, refs, y_ref, consts in dirs:
            sub = nsub - 1 - step if rev else step
            rows = slice(sub * c, (sub + 1) * c)
            for bi in range(nb):
                for q in range(nquad):
                    lanes = slice(q * QUAD, (q + 1) * QUAD)
                    ins = (x[bi, rows, lanes].astype(F32) for x in refs)
                    chains.append(_scan_chain(*ins, s_scr.at[bi, d, q], y_ref.at[bi, rows, lanes], rev, consts))
        while chains:
            for ch in list(chains):
                if next(ch, _DONE) is _DONE:
                    chains.remove(ch)

    @pl.when(i == n - 1)
    def _():
        sout_ref[...] = s_scr[...]


def _scan_call(r, v, a, lw_f, lw_b, k_f, k_b, b_f, b_b, s0):
    b, l, w = r.shape
    rows = SCAN_SUB * CHUNK if l % (SCAN_SUB * CHUNK) == 0 else CHUNK
    nc = l // rows
    fwd = pl.BlockSpec((b, rows, w), lambda i: (0, i, 0))
    bwd = pl.BlockSpec((b, rows, w), lambda i: (0, nc - 1 - i, 0))
    st = pl.BlockSpec(s0.shape, lambda i: (0, 0, 0, 0, 0))
    sds = jax.ShapeDtypeStruct((b, l, w), F32)
    return pl.pallas_call(
        _scan_kernel, grid=(nc,),
        in_specs=[fwd] * 6 + [bwd] * 6 + [st],
        out_specs=[fwd, bwd, st],
        out_shape=[sds, sds, jax.ShapeDtypeStruct(s0.shape, F32)],
        scratch_shapes=[pltpu.VMEM(s0.shape, F32)],
        compiler_params=_params(("arbitrary",)),
        name="scan",
    )(r, v, a, lw_f, k_f, b_f, r, v, a, lw_b, k_b, b_b, s0)


ONES_ROWS = 16
QBLK = 256


def _attn_kernel(zero_ref, q_ref, k_ref, kc_ref, vt_ref, vtc_ref, o_ref, sa_scr, sb_scr, *, r):
    tq = q_ref.shape[1]
    tv = vt_ref.shape[3]
    tk = r * tv
    n_lat = vt_ref.shape[1]
    nk = (n_lat + vtc_ref.shape[1]) // r
    qs = (q_ref[0, :, 0:QK_PAD], q_ref[0, :, QK_PAD:2 * QK_PAD])
    ones = jnp.ones((ONES_ROWS, tk), BF16)
    z0 = pl.multiple_of(zero_ref[0] * 8, 8)

    qb = min(QBLK, tq)
    nqb = tq // qb
    heads, blocks = range(2), range(nqb)

    def keys(j):
        lo, hi = j * r, (j + 1) * r
        parts = []
        if lo < n_lat:
            parts.append(k_ref[0, lo * tv:min(hi, n_lat) * tv, :])
        if hi > n_lat:
            parts.append(kc_ref[0, (max(lo, n_lat) - n_lat) * tv:(hi - n_lat) * tv, :])
        kt = jnp.concatenate(parts, axis=0) if len(parts) > 1 else parts[0]
        return [kt[:, h * QK_PAD:(h + 1) * QK_PAD] for h in heads]

    def value_tile(idx, h):
        ref, i = (vt_ref, idx) if idx < n_lat else (vtc_ref, idx - n_lat)
        return ref[0, i, h * V_HEAD:(h + 1) * V_HEAD, :]

    def produce(kt, s_scr, h, n):
        s = _dot(kt[h], qs[h][n * qb:(n + 1) * qb], NT)
        s_scr[h, pl.ds(z0, tk), n * qb:(n + 1) * qb] = s
        return jnp.max(s, axis=0, keepdims=True)

    def consume(vaug, s_scr, h, n, cmax, m, acc):
        mn = jnp.maximum(m, cmax)
        p = jnp.exp2(s_scr[h, pl.ds(z0, tk), n * qb:(n + 1) * qb] - mn).astype(BF16)
        return mn, jnp.exp2(m - mn) * acc + _dot(vaug, p)

    neg = jnp.full((1, qb), -1e30, F32)
    zacc = jnp.zeros((V_HEAD + ONES_ROWS, qb), F32)
    m = [[neg for _ in blocks] for _ in heads]
    acc = [[zacc for _ in blocks] for _ in heads]
    bufs = (sa_scr, sb_scr)
    kt = keys(0)
    cmax = [[produce(kt, bufs[0], h, n) for n in blocks] for h in heads]
    for j in range(nk):
        vaug = []
        for h in heads:
            vt = [value_tile(j * r + t, h) for t in range(r)]
            vaug.append(jnp.concatenate([jnp.concatenate(vt, axis=1) if r > 1 else vt[0], ones], axis=0))
        kt = keys(j + 1) if j + 1 < nk else None
        cmax_next = [[None for _ in blocks] for _ in heads]
        for n in blocks:
            for h in heads:
                if kt is not None:
                    cmax_next[h][n] = produce(kt, bufs[(j + 1) % 2], h, n)
                m[h][n], acc[h][n] = consume(vaug[h], bufs[j % 2], h, n, cmax[h][n], m[h][n], acc[h][n])
        cmax = cmax_next
    ot = jnp.concatenate([jnp.concatenate([a[0:V_HEAD] / a[V_HEAD:V_HEAD + 1] for a in acc[h]], axis=1)
                          for h in heads], axis=0)
    o_ref[0] = ot.T.astype(o_ref.dtype)


def _attn_call(q, k, kc, vt, vtc, *, tq, r):
    b, l, _ = q.shape
    tv = vt.shape[3]
    npair = MLA_HEADS // 2
    return pl.pallas_call(
        functools.partial(_attn_kernel, r=r),
        grid=(b, npair, l // tq),
        in_specs=[pl.BlockSpec(memory_space=pltpu.SMEM),
                  pl.BlockSpec((1, tq, 2 * QK_PAD), lambda bb, p, i: (bb, i, p)),
                  pl.BlockSpec((1, k.shape[1], 2 * QK_PAD), lambda bb, p, i: (bb, 0, p)),
                  pl.BlockSpec((1, kc.shape[1], 2 * QK_PAD), lambda bb, p, i: (bb, 0, p)),
                  pl.BlockSpec((1, vt.shape[1], LANES, tv), lambda bb, p, i: (bb, 0, p, 0)),
                  pl.BlockSpec((1, vtc.shape[1], LANES, tv), lambda bb, p, i: (bb, 0, p, 0))],
        out_specs=pl.BlockSpec((1, tq, LANES), lambda bb, p, i: (bb, i, p)),
        out_shape=jax.ShapeDtypeStruct((b, l, MLA_WIDTH), BF16),
        scratch_shapes=[pltpu.VMEM((2, r * tv, tq), F32), pltpu.VMEM((2, r * tv, tq), F32)],
        compiler_params=_params(("parallel", "parallel", "arbitrary")),
        name="attn",
    )(jnp.zeros((1,), jnp.int32), q, k, kc, vt, vtc)


def _outproj_kernel(x_ref, mod_ref, yf_ref, yb_ref, bon_ref, grw_ref, o_ref, gmla_ref, lg_ref, lb_ref, ones_ref,
                    wo_ref, fg_ref, out_ref):
    x = x_ref[0]
    gate = mod_ref[0, 2:3, :]
    ones = ones_ref[...]
    y = yf_ref[0] + yb_ref[0]
    inv_n = 1.0 / RW_HEAD
    mean = _mm_exact_rhs(y, ones, 2) * inv_n
    yc = y - mean
    var = _mm_exact_rhs(yc * yc, ones, 1) * inv_n
    yn = yc * lax.rsqrt(var + LNX_EPS) * lg_ref[...] + lb_ref[...]
    rw = (yn + bon_ref[0].astype(F32)) * grw_ref[0].astype(F32)
    mla = o_ref[0].astype(F32) * gmla_ref[0].astype(F32)
    cat = jnp.concatenate([rw, mla], axis=1).astype(BF16)
    z = x + gate * _dot(cat, wo_ref[...])
    out_ref[0] = z * lax.rsqrt(jnp.mean(z * z, axis=-1, keepdims=True) + NORM_EPS) * fg_ref[...]


def _outproj_call(x, mod3, y_f, y_b, bonus, sg_rw, o, sg_mla, lnx_g, lnx_b, ones_blk, w_out, final_g, *, tm):
    b, l, d = x.shape
    const = lambda bb, i: (0, 0)
    row = lambda w: pl.BlockSpec((1, tm, w), lambda bb, i: (bb, i, 0))
    return pl.pallas_call(
        _outproj_kernel, grid=(b, l // tm),
        in_specs=[row(d), pl.BlockSpec((1, 3, d), lambda bb, i: (bb, 0, 0)),
                  row(RW_WIDTH), row(RW_WIDTH), row(RW_WIDTH), row(RW_WIDTH), row(MLA_WIDTH), row(MLA_WIDTH),
                  pl.BlockSpec(lnx_g.shape, const), pl.BlockSpec(lnx_b.shape, const),
                  pl.BlockSpec(ones_blk.shape, const), pl.BlockSpec(w_out.shape, const),
                  pl.BlockSpec(final_g.shape, const)],
        out_specs=row(d),
        out_shape=jax.ShapeDtypeStruct((b, l, d), F32),
        compiler_params=_params(("parallel", "arbitrary")),
        name="outproj",
    )(x, mod3, y_f, y_b, bonus, sg_rw, o, sg_mla, lnx_g, lnx_b, ones_blk, w_out, final_g)


def _rope_tables(l):
    rows = l // GRID_W
    inv_freq = ROPE_THETA ** (-jnp.arange(0, AXIS_DIM, 2, dtype=F32) / AXIS_DIM)
    ang_r = jnp.arange(rows, dtype=F32)[:, None] * inv_freq
    ang_c = jnp.arange(GRID_W, dtype=F32)[:, None] * inv_freq

    half = AXIS_DIM // 2
    cos_r = jnp.concatenate([jnp.ones((rows, QK_NOPE), F32), jnp.cos(ang_r), jnp.cos(ang_r),
                             jnp.ones((rows, LANES - QK_NOPE - 2 * half), F32)], axis=1)
    cos_c = jnp.concatenate([jnp.ones((GRID_W, QK_NOPE + 2 * half), F32), jnp.cos(ang_c), jnp.cos(ang_c),
                             jnp.ones((GRID_W, LANES - QK_DIM), F32)], axis=1)
    sin_r = jnp.concatenate([jnp.zeros((rows, QK_NOPE), F32), -jnp.sin(ang_r), jnp.sin(ang_r),
                             jnp.zeros((rows, LANES - QK_NOPE - 2 * half), F32)], axis=1)
    sin_c = jnp.concatenate([jnp.zeros((GRID_W, QK_NOPE + 2 * half), F32), -jnp.sin(ang_c), jnp.sin(ang_c),
                             jnp.zeros((GRID_W, LANES - QK_DIM), F32)], axis=1)
    cos_t = (cos_r[:, None, :] * cos_c[None, :, :]).reshape(l, LANES)
    sin_t = (sin_r[:, None, :] + sin_c[None, :, :]).reshape(l, LANES)
    return cos_t, sin_t


def kernel(x, c, ctx, c_ctx, ada_w, ada_b, norm_g, w_in, shift_mu, rw_w0, rw_w2, rw_a0, rw_a2, rw_kk, rw_ka, rw_rk,
           rw_lnx_g, rw_lnx_b, mla_q_norm_g, mla_kv_norm_g, mla_w_uq, mla_w_ukv, w_out, final_g):
    assert x.shape[-1] == 2 * RW_WIDTH and w_in.shape[0] == 1, "single-layer block with d_model = 1024"
    b, l, d = x.shape
    lc = ctx.shape[1]

    w = w_in[0]
    o1 = N_SHIFT + RW_WIDTH
    o2 = o1 + Q_LORA + KV_LORA + QK_ROPE
    w_kr = jnp.zeros((d, LANES), F32).at[:, QK_NOPE:QK_DIM].set(w[:, o1 + Q_LORA + KV_LORA:o2])
    w_cat = jnp.concatenate([w[:, :N_SHIFT], w[:, N_SHIFT:o1], w[:, o2:], w[:, o1:o1 + Q_LORA],
                             w[:, o1 + Q_LORA:o1 + Q_LORA + KV_LORA], w_kr], axis=1).astype(BF16)
    wuq = jnp.pad(mla_w_uq[0].reshape(Q_LORA, MLA_HEADS, QK_DIM), ((0, 0), (0, 0), (0, QK_PAD - QK_DIM)))
    wuq = wuq.reshape(Q_LORA, MLA_HEADS * QK_PAD).astype(BF16)
    wukv = mla_w_ukv[0].reshape(KV_LORA, MLA_HEADS, QK_NOPE + V_HEAD)
    wk = jnp.pad(wukv[:, :, :QK_NOPE], ((0, 0), (0, 0), (0, QK_PAD - QK_NOPE))).reshape(KV_LORA, MLA_HEADS * QK_PAD).astype(BF16)
    wv = wukv[:, :, QK_NOPE:].reshape(KV_LORA, MLA_WIDTH).T.astype(BF16)
    zl = jnp.zeros((DECAY_LORA, RW_WIDTH), F32)
    w2bd = jnp.concatenate([jnp.concatenate([rw_w2[0, 0], zl], axis=1), jnp.concatenate([zl, rw_w2[0, 1]], axis=1)], axis=0)
    a2bd = jnp.concatenate([jnp.concatenate([rw_a2[0, 0], zl], axis=1), jnp.concatenate([zl, rw_a2[0, 1]], axis=1)], axis=0)
    w0 = rw_w0[0].reshape(1, 2 * RW_WIDTH)
    a0 = rw_a0[0].reshape(1, 2 * RW_WIDTH)
    hid = np.arange(RW_WIDTH) // RW_HEAD
    ones_blk = jnp.asarray(hid[:, None] == hid[None, :], BF16)
    cos_t, sin_t = _rope_tables(l)
    row1 = lambda t: t.reshape(1, -1)

    c_rows = jnp.zeros((8, d), F32).at[:b].set(c).at[b].set(c_ctx)
    mod = _mod_call(c_rows, ada_w[0], ada_b[0])
    mod_lat = mod[:b].reshape(b, 3, d)
    mod_ctx = jnp.broadcast_to(mod[b].reshape(1, 3, d), (b, 3, d))

    consts = (row1(norm_g[0]), w_cat, row1(mla_q_norm_g[0]), row1(mla_kv_norm_g[0]), wuq, wk, wv,
              shift_mu[0], w0, w2bd, a0, a2bd, row1(rw_kk[0]), row1(rw_ka[0]), row1(rw_rk[0]), ones_blk)
    tm_f = min(256, l, lc)
    front = functools.partial(_front_call, consts=consts, cos_t=cos_t, sin_t=sin_t, tm=tm_f)
    sg_rw, sg_mla, q, k_lat, vt_lat, *pz = front(x, mod_lat, rope=True, latent=True)
    k_ctx, vt_ctx, *pc = front(ctx, mod_ctx, rope=False, latent=False)

    s_zero = jnp.zeros((b, 2, RW_WIDTH // QUAD, QUAD, QUAD), F32)
    _, _, s_ctx = _scan_call(*pc[:9], s_zero)
    y_f, y_b, _ = _scan_call(*pz[:9], s_ctx)

    r = next(t for t in (3, 2, 1) if (vt_lat.shape[1] + vt_ctx.shape[1]) % t == 0)
    tq = next(t for t in (1024, 512, 256, l) if l % t == 0)
    o = _attn_call(q, k_lat, k_ctx, vt_lat, vt_ctx, tq=tq, r=r)

    return _outproj_call(x, mod_lat, y_f, y_b, pz[9], sg_rw, o, sg_mla, row1(rw_lnx_g[0]), row1(rw_lnx_b[0]),
                         ones_blk, w_out[0].astype(BF16), row1(final_g), tm=min(512, l))
```
